```python
import math
import jax, jax.numpy as jnp
from jax import lax
import numpy as np

D_MODEL = 1024
BATCH = 2
SEQ = 16384
DEPTH = 2

N_MIXERS = 2
N_MLA_LAYERS = (DEPTH + 1) // N_MIXERS
N_HYENA_LAYERS = DEPTH // N_MIXERS

MLA_HEADS = 8
QK_NOPE_DIM = 128
QK_ROPE_DIM = 64
V_HEAD_DIM = 128
Q_LORA_RANK = 384
KV_LORA_RANK = 256
MLA_IN_DIM = Q_LORA_RANK + KV_LORA_RANK + QK_ROPE_DIM
ROPE_THETA = 10000.0
Q_BLOCK = 128

HYENA_ORDER = 2
HYENA_EMB_DIM = 33
HYENA_FILTER_WIDTH = 64
HYENA_FAST_DECAY = 0.3
HYENA_SLOW_DECAY = 1.5
HYENA_TARGET = 1e-2
SHORT_CONV_WIDTH = 3

D_FF = 2816
FFN_CONV_WIDTH = 3

NORM_EPS = 1e-5
RMS_EPS = 1e-6
DEEPNORM_ALPHA = (2.0 * DEPTH) ** 0.25
DEEPNORM_BETA = (8.0 * DEPTH) ** -0.25

kernel_name = "hybrid_mla_hyena_convffn_deepnorm"


def layer_norm(x, g, b):
    xf = x.astype(jnp.float32)
    mu = jnp.mean(xf, -1, keepdims=True)
    var = jnp.mean(jnp.square(xf - mu), -1, keepdims=True)
    return ((xf - mu) * lax.rsqrt(var + NORM_EPS) * g.astype(jnp.float32) + b.astype(jnp.float32)).astype(x.dtype)


def rms_norm(x, g):
    xf = x.astype(jnp.float32)
    ms = jnp.mean(jnp.square(xf), -1, keepdims=True)
    return (xf * lax.rsqrt(ms + RMS_EPS) * g.astype(jnp.float32)).astype(x.dtype)


def dwconv_centred(x, w):
    k_width = w.shape[0]
    pad = k_width // 2
    s = x.shape[1]
    xp = jnp.pad(x, ((0, 0), (pad, pad), (0, 0)))
    return sum(xp[:, k:k + s] * w[k] for k in range(k_width))


def rope_tables(positions):
    inv = 1.0 / (ROPE_THETA ** (jnp.arange(0, QK_ROPE_DIM, 2, dtype=jnp.float32) / QK_ROPE_DIM))
    ang = positions.astype(jnp.float32)[..., None] * inv
    return jnp.cos(ang), jnp.sin(ang)


def apply_rope(x, cos, sin):
    x1, x2 = jnp.split(x.astype(jnp.float32), 2, axis=-1)
    return jnp.concatenate([x1 * cos - x2 * sin, x1 * sin + x2 * cos], -1).astype(x.dtype)


def mla_mixer(x, cos, sin, w_in, g_q, w_uq, g_kv, w_ukv, w_o):
    b, s, _ = x.shape
    h = x @ w_in
    c_q, c_kv, k_rope = jnp.split(h, [Q_LORA_RANK, Q_LORA_RANK + KV_LORA_RANK], axis=-1)
    q = (rms_norm(c_q, g_q) @ w_uq).reshape(b, s, MLA_HEADS, QK_NOPE_DIM + QK_ROPE_DIM)
    q_nope = q[..., :QK_NOPE_DIM]
    q_rope = apply_rope(q[..., QK_NOPE_DIM:], cos[:, :, None], sin[:, :, None])
    k_rope = apply_rope(k_rope, cos, sin)
    kv = (rms_norm(c_kv, g_kv) @ w_ukv).reshape(b, s, MLA_HEADS, QK_NOPE_DIM + V_HEAD_DIM)
    k_nope, v = kv[..., :QK_NOPE_DIM], kv[..., QK_NOPE_DIM:]
    scale = (QK_NOPE_DIM + QK_ROPE_DIM) ** -0.5
    nb = s // Q_BLOCK
    qn_blocks = q_nope.reshape(b, nb, Q_BLOCK, MLA_HEADS, QK_NOPE_DIM).transpose(1, 0, 2, 3, 4)
    qr_blocks = q_rope.reshape(b, nb, Q_BLOCK, MLA_HEADS, QK_ROPE_DIM).transpose(1, 0, 2, 3, 4)

    def attend(blk):
        qn, qr = blk
        sc = (jnp.einsum('bqhd,bkhd->bhqk', qn, k_nope)
              + jnp.einsum('bqhr,bkr->bhqk', qr, k_rope))
        p = jax.nn.softmax(sc.astype(jnp.float32) * scale, axis=-1).astype(v.dtype)
        return jnp.einsum('bhqk,bkhd->bqhd', p, v)

    o = lax.map(attend, (qn_blocks, qr_blocks))
    o = o.transpose(1, 0, 2, 3, 4).reshape(b, s, MLA_HEADS * V_HEAD_DIM)
    return o @ w_o


def hyena_positional_features(length):
    t = jnp.linspace(0.0, 1.0, length, dtype=jnp.float32)[:, None]
    bands = (HYENA_EMB_DIM - 1) // 2
    w = 2.0 * math.pi * jnp.arange(length, dtype=jnp.float32)[:, None] / length
    f = jnp.linspace(1e-4, bands - 1, bands, dtype=jnp.float32)[None, :]
    z = jnp.concatenate([t, jnp.cos(f * w), -jnp.sin(f * w)], axis=-1)
    return t, z


def hyena_filters(length, fw1, fb1, fw2, fb2, fw3, fb3, freq, fw_out):
    t, z = hyena_positional_features(length)
    freq = freq.astype(jnp.float32)
    h = jnp.sin(freq * (z @ fw1.astype(jnp.float32) + fb1.astype(jnp.float32)))
    h = jnp.sin(freq * (h @ fw2.astype(jnp.float32) + fb2.astype(jnp.float32)))
    h = jnp.sin(freq * (h @ fw3.astype(jnp.float32) + fb3.astype(jnp.float32)))
    h = h @ fw_out.astype(jnp.float32)
    deltas = jnp.abs(jnp.linspace(math.log(HYENA_FAST_DECAY) / HYENA_TARGET,
                                  math.log(HYENA_SLOW_DECAY) / HYENA_TARGET,
                                  D_MODEL, dtype=jnp.float32))
    decay = jnp.exp(-t * deltas)
    return h.reshape(length, HYENA_ORDER, 2, D_MODEL) * decay[:, None, None, :]


def two_sided_kernel(h_fwd, h_bwd):
    zero = jnp.zeros_like(h_fwd[:1])
    return jnp.concatenate([h_fwd.at[0].add(h_bwd[0]), zero, h_bwd[:0:-1]], axis=0)


def hyena_mixer(x, w_in, w_short, fw1, fb1, fw2, fb2, fw3, fb3, freq, fw_out, d_bias, w_o):
    b, length, _ = x.shape
    u = dwconv_centred(x @ w_in, w_short)
    v, g1, g2 = jnp.split(u, 3, axis=-1)
    filt = hyena_filters(length, fw1, fb1, fw2, fb2, fw3, fb3, freq, fw_out)
    n = 2 * length
    z = v.astype(jnp.float32)
    for o, gate in enumerate((g1, g2)):
        k_f = jnp.fft.rfft(two_sided_kernel(filt[:, o, 0], filt[:, o, 1]), n=n, axis=0)
        z_f = jnp.fft.rfft(z, n=n, axis=1)
        y = jnp.fft.irfft(z_f * k_f, n=n, axis=1)[:, :length]
        z = gate.astype(jnp.float32) * (y + d_bias[o].astype(jnp.float32) * z)
    return z.astype(x.dtype) @ w_o


def conv_ffn(x, w_up, w_conv, w_down):
    h = dwconv_centred(x @ w_up, w_conv)
    a, g = jnp.split(h, 2, axis=-1)
    return (jax.nn.silu(g) * a) @ w_down


def setup_inputs(seed: int = 0) -> dict:
    key = jax.random.key(seed)
    ks = iter(jax.random.split(key, 40))

    def nrm(shape, scale):
        return jax.random.normal(next(ks), shape, jnp.float32) * scale

    def gain(shape):
        return 1.0 + nrm(shape, 0.01)

    nm, nh = N_MLA_LAYERS, N_HYENA_LAYERS
    hf = HYENA_FILTER_WIDTH
    return {
        "x": nrm((BATCH, SEQ, D_MODEL), 1.0),
        "positions": jnp.broadcast_to(jnp.arange(SEQ, dtype=jnp.int32), (BATCH, SEQ)),
        "mla_w_in": nrm((nm, D_MODEL, MLA_IN_DIM), D_MODEL ** -0.5),
        "mla_g_q": gain((nm, Q_LORA_RANK)),
        "mla_w_uq": nrm((nm, Q_LORA_RANK, MLA_HEADS * (QK_NOPE_DIM + QK_ROPE_DIM)), Q_LORA_RANK ** -0.5),
        "mla_g_kv": gain((nm, KV_LORA_RANK)),
        "mla_w_ukv": nrm((nm, KV_LORA_RANK, MLA_HEADS * (QK_NOPE_DIM + V_HEAD_DIM)), KV_LORA_RANK ** -0.5),
        "mla_w_o": nrm((nm, MLA_HEADS * V_HEAD_DIM, D_MODEL), (MLA_HEADS * V_HEAD_DIM) ** -0.5 * DEEPNORM_BETA),
        "hy_w_in": nrm((nh, D_MODEL, 3 * D_MODEL), D_MODEL ** -0.5),
        "hy_w_short": nrm((nh, SHORT_CONV_WIDTH, 3 * D_MODEL), SHORT_CONV_WIDTH ** -0.5),
        "hy_fw1": nrm((nh, HYENA_EMB_DIM, hf), HYENA_EMB_DIM ** -0.5),
        "hy_fb1": nrm((nh, hf), 0.02),
        "hy_fw2": nrm((nh, hf, hf), hf ** -0.5),
        "hy_fb2": nrm((nh, hf), 0.02),
        "hy_fw3": nrm((nh, hf, hf), hf ** -0.5),
        "hy_fb3": nrm((nh, hf), 0.02),
        "hy_freq": gain((nh, hf)),
        "hy_fw_out": nrm((nh, hf, HYENA_ORDER * 2 * D_MODEL), hf ** -0.5),
        "hy_d_bias": nrm((nh, HYENA_ORDER, D_MODEL), 0.1),
        "hy_w_o": nrm((nh, D_MODEL, D_MODEL), D_MODEL ** -0.5 * DEEPNORM_BETA),
        "ffn_w_up": nrm((DEPTH, D_MODEL, 2 * D_FF), D_MODEL ** -0.5),
        "ffn_w_conv": nrm((DEPTH, FFN_CONV_WIDTH, 2 * D_FF), FFN_CONV_WIDTH ** -0.5),
        "ffn_w_down": nrm((DEPTH, D_FF, D_MODEL), D_FF ** -0.5 * DEEPNORM_BETA),
        "ln1_g": gain((DEPTH, D_MODEL)),
        "ln1_b": nrm((DEPTH, D_MODEL), 0.01),
        "ln2_g": gain((DEPTH, D_MODEL)),
        "ln2_b": nrm((DEPTH, D_MODEL), 0.01),
    }


def reference(x, positions, mla_w_in, mla_g_q, mla_w_uq, mla_g_kv, mla_w_ukv, mla_w_o,
              hy_w_in, hy_w_short, hy_fw1, hy_fb1, hy_fw2, hy_fb2, hy_fw3, hy_fb3, hy_freq,
              hy_fw_out, hy_d_bias, hy_w_o, ffn_w_up, ffn_w_conv, ffn_w_down,
              ln1_g, ln1_b, ln2_g, ln2_b):
    cos, sin = rope_tables(positions)
    for i in range(DEPTH):
        j = i // N_MIXERS
        if i % N_MIXERS == 0:
            m = mla_mixer(x, cos, sin, mla_w_in[j], mla_g_q[j], mla_w_uq[j],
                          mla_g_kv[j], mla_w_ukv[j], mla_w_o[j])
        else:
            m = hyena_mixer(x, hy_w_in[j], hy_w_short[j], hy_fw1[j], hy_fb1[j], hy_fw2[j],
                            hy_fb2[j], hy_fw3[j], hy_fb3[j], hy_freq[j], hy_fw_out[j],
                            hy_d_bias[j], hy_w_o[j])
        x = layer_norm(DEEPNORM_ALPHA * x + m, ln1_g[i], ln1_b[i])
        x = layer_norm(DEEPNORM_ALPHA * x + conv_ffn(x, ffn_w_up[i], ffn_w_conv[i], ffn_w_down[i]),
                       ln2_g[i], ln2_b[i])
    return x
```

```python
import functools
import math

import numpy as np
import jax
import jax.numpy as jnp
from jax import lax
from jax.experimental import pallas as pl
from jax.experimental.pallas import tpu as pltpu

_F32 = jnp.float32
_BF16 = jnp.bfloat16

_HEADS = 8
_NOPE = 128
_ROPE = 64
_VDIM = 128
_QK = _NOPE + _ROPE
_Q_LORA = 384
_KV_LORA = 256
_ROPE_THETA = 10000.0
_HY_EMB = 33
_HY_BANDS = (_HY_EMB - 1) // 2
_HY_FAST, _HY_SLOW, _HY_TARGET = 0.3, 1.5, 1e-2
_NORM_EPS = 1e-5
_RMS_EPS = 1e-6
_DEPTH = 2
_ALPHA = (2.0 * _DEPTH) ** 0.25

_LANES = 128
_SUBLANES = 8
_VMEM_LIMIT = 56 * 1024 * 1024

_N2 = 128


def _params(*sem):
    return pltpu.CompilerParams(dimension_semantics=sem, vmem_limit_bytes=_VMEM_LIMIT)


def _const_spec(shape):
    nd = len(shape)
    return pl.BlockSpec(shape, lambda *_: (0,) * nd, pipeline_mode=pl.Buffered(1))


def _dot(a, b):
    return jnp.dot(a, b, preferred_element_type=_F32)


def _layer_norm(y, g, b):
    mu = jnp.mean(y, axis=-1, keepdims=True)
    yc = y - mu
    var = jnp.mean(yc * yc, axis=-1, keepdims=True)
    return yc * lax.rsqrt(var + _NORM_EPS) * g + b


def _mla_proj_kernel(x_ref, pos_ref, win_ref, gq_ref, gkv_ref, wq_ref, wkv_ref, invf_ref, sgn_ref,
                     q_ref, k_ref, v_ref, *, qscale):
    xb = x_ref[...].astype(_BF16)
    h = _dot(xb, win_ref[...])
    cq = h[:, :_Q_LORA]
    ckv = h[:, _Q_LORA:_Q_LORA + _KV_LORA]
    kr = h[:, 640:704]
    krs = h[:, 768:832]

    def rms(c, g):
        ms = jnp.mean(c * c, axis=-1, keepdims=True)
        return (c * lax.rsqrt(ms + _RMS_EPS) * g).astype(_BF16)

    q = _dot(rms(cq, gq_ref[...]), wq_ref[...])
    kv = _dot(rms(ckv, gkv_ref[...]), wkv_ref[...])

    ang = pos_ref[...].astype(_F32) * invf_ref[...]
    c128 = jnp.cos(ang)
    s128 = jnp.sin(ang) * sgn_ref[...]
    c512 = jnp.concatenate([c128] * 4, axis=1)
    s512 = jnp.concatenate([s128] * 4, axis=1)
    nh = _HEADS * _NOPE
    nr = _HEADS * _ROPE
    q_nope = q[:, :nh] * qscale
    q_rope = (q[:, nh:nh + nr] * c512 + q[:, nh + nr:] * s512) * qscale
    k_rope = (kr * c128[:, :_ROPE] + krs * s128[:, :_ROPE]).astype(_BF16)
    for hd in range(_HEADS):
        q_ref[0, hd, :, :_NOPE] = q_nope[:, hd * _NOPE:(hd + 1) * _NOPE].astype(_BF16)
        q_ref[0, hd, :, _NOPE:] = q_rope[:, hd * _ROPE:(hd + 1) * _ROPE].astype(_BF16)
        k_ref[0, hd, :, :_NOPE] = kv[:, hd * _NOPE:(hd + 1) * _NOPE].astype(_BF16)
        k_ref[0, hd, :, _NOPE:] = k_rope
        v_ref[0, hd] = kv[:, nh + hd * _VDIM:nh + (hd + 1) * _VDIM].astype(_BF16)


def _mla_proj(x2d, pos2d, win, gq, gkv, wq, wkv, invf, sgn, batch, seq, tm):
    t, d = x2d.shape
    spb = seq // tm
    qscale = (_QK ** -0.5) * math.log2(math.e)

    def head_spec(width):
        return pl.BlockSpec((1, _HEADS, tm, width), lambda i: (i // spb, 0, i % spb, 0))

    return pl.pallas_call(
        functools.partial(_mla_proj_kernel, qscale=qscale),
        grid=(t // tm,),
        in_specs=[
            pl.BlockSpec((tm, d), lambda i: (i, 0)),
            pl.BlockSpec((tm, 1), lambda i: (i, 0)),
            _const_spec(win.shape), _const_spec(gq.shape), _const_spec(gkv.shape),
            _const_spec(wq.shape), _const_spec(wkv.shape), _const_spec(invf.shape), _const_spec(sgn.shape),
        ],
        out_specs=[head_spec(_QK), head_spec(_QK), head_spec(_VDIM)],
        out_shape=[
            jax.ShapeDtypeStruct((batch, _HEADS, seq, _QK), _BF16),
            jax.ShapeDtypeStruct((batch, _HEADS, seq, _QK), _BF16),
            jax.ShapeDtypeStruct((batch, _HEADS, seq, _VDIM), _BF16),
        ],
        compiler_params=_params("parallel"),
        name="mla_proj",
    )(x2d, pos2d, win, gq, gkv, wq, wkv, invf, sgn)


def _attn_kernel(q_ref, k_ref, v_ref, o_ref, m_ref, l_ref, acc_ref):
    ki = pl.program_id(3)

    @pl.when(ki == 0)
    def _():
        m_ref[...] = jnp.full(m_ref.shape, -jnp.inf, _F32)
        l_ref[...] = jnp.zeros(l_ref.shape, _F32)
        acc_ref[...] = jnp.zeros(acc_ref.shape, _F32)

    s = lax.dot_general(q_ref[0, 0], k_ref[0, 0], (((1,), (1,)), ((), ())),
                        preferred_element_type=_F32)
    m_prev = m_ref[...]
    m_new = jnp.maximum(m_prev, jnp.max(s, axis=1, keepdims=True))
    alpha = jnp.exp2(m_prev - m_new)
    p = jnp.exp2(s - m_new)
    l_ref[...] = alpha * l_ref[...] + jnp.sum(p, axis=1, keepdims=True)
    acc_ref[...] = alpha * acc_ref[...] + _dot(p.astype(_BF16), v_ref[0, 0])
    m_ref[...] = m_new

    @pl.when(ki == pl.num_programs(3) - 1)
    def _():
        o_ref[0] = (acc_ref[...] / l_ref[...]).astype(o_ref.dtype)


def _attention(q, k, v, tq, tk):
    b, h, s, _ = q.shape
    return pl.pallas_call(
        _attn_kernel,
        grid=(b, h, s // tq, s // tk),
        in_specs=[
            pl.BlockSpec((1, 1, tq, _QK), lambda bi, hi, qi, ki: (bi, hi, qi, 0)),
            pl.BlockSpec((1, 1, tk, _QK), lambda bi, hi, qi, ki: (bi, hi, ki, 0)),
            pl.BlockSpec((1, 1, tk, _VDIM), lambda bi, hi, qi, ki: (bi, hi, ki, 0)),
        ],
        out_specs=pl.BlockSpec((1, tq, _VDIM), lambda bi, hi, qi, ki: (bi, qi, hi)),
        out_shape=jax.ShapeDtypeStruct((b, s, h * _VDIM), _BF16),
        scratch_shapes=[
            pltpu.VMEM((tq, 1), _F32),
            pltpu.VMEM((tq, 1), _F32),
            pltpu.VMEM((tq, _VDIM), _F32),
        ],
        compiler_params=_params("parallel", "parallel", "parallel", "arbitrary"),
        name="mla_attention",
    )(q, k, v)


def _proj_ln_kernel(a_ref, x_ref, w_ref, g_ref, b_ref, o_ref):
    m = _dot(a_ref[...].astype(_BF16), w_ref[...])
    o_ref[...] = _layer_norm(_ALPHA * x_ref[...] + m, g_ref[...], b_ref[...])


def _proj_ln(a2d, x2d, w, g, b, tm):
    t, d = x2d.shape
    ka = a2d.shape[1]
    return pl.pallas_call(
        _proj_ln_kernel,
        grid=(t // tm,),
        in_specs=[
            pl.BlockSpec((tm, ka), lambda i: (i, 0)),
            pl.BlockSpec((tm, d), lambda i: (i, 0)),
            _const_spec(w.shape), _const_spec(g.shape), _const_spec(b.shape),
        ],
        out_specs=pl.BlockSpec((tm, d), lambda i: (i, 0)),
        out_shape=jax.ShapeDtypeStruct((t, d), _F32),
        compiler_params=_params("parallel"),
        name="proj_ln",
    )(a2d, x2d, w, g, b)


def _halo_specs(tm, d, nrows):
    r = tm // _SUBLANES
    last = nrows // _SUBLANES - 1
    return [
        pl.BlockSpec((_SUBLANES, d), lambda i: (jnp.maximum(i * r - 1, 0), 0)),
        pl.BlockSpec((tm, d), lambda i: (i, 0)),
        pl.BlockSpec((_SUBLANES, d), lambda i: (jnp.minimum((i + 1) * r, last), 0)),
    ]


def _assemble_halo(xp_ref, x_ref, xn_ref, xcat_ref, tm, tiles_per_seq):
    i = pl.program_id(0)
    first = (i % tiles_per_seq) == 0
    last = (i % tiles_per_seq) == tiles_per_seq - 1
    xcat_ref[0:_SUBLANES, :] = jnp.where(first, 0.0, xp_ref[...])
    xcat_ref[_SUBLANES:_SUBLANES + tm, :] = x_ref[...]
    xcat_ref[_SUBLANES + tm:, :] = jnp.where(last, 0.0, xn_ref[...])


def _dwconv3(h_ref, wc_ref, tm):
    lo = h_ref[pl.ds(_SUBLANES - 1, tm), :]
    mid = h_ref[pl.ds(_SUBLANES, tm), :]
    hi = h_ref[pl.ds(_SUBLANES + 1, tm), :]
    return lo * wc_ref[0:1, :] + mid * wc_ref[1:2, :] + hi * wc_ref[2:3, :]


def _conv_ffn_kernel(xp_ref, x_ref, xn_ref, wup_ref, wc_ref, wdn_ref, g_ref, b_ref, o_ref,
                     xcat_ref, h_ref, *, tm, tiles_per_seq, dff):
    _assemble_halo(xp_ref, x_ref, xn_ref, xcat_ref, tm, tiles_per_seq)
    h_ref[...] = _dot(xcat_ref[...].astype(_BF16), wup_ref[...])
    hc = _dwconv3(h_ref, wc_ref, tm)
    a = hc[:, :dff]
    gt = hc[:, dff:]
    act = (gt * jax.nn.sigmoid(gt) * a).astype(_BF16)
    f = _dot(act, wdn_ref[...])
    o_ref[...] = _layer_norm(_ALPHA * x_ref[...] + f, g_ref[...], b_ref[...])


def _conv_ffn(x2d, wup, wc, wdn, g, b, seq, tm):
    t, d = x2d.shape
    dff = wdn.shape[0]
    return pl.pallas_call(
        functools.partial(_conv_ffn_kernel, tm=tm, tiles_per_seq=seq // tm, dff=dff),
        grid=(t // tm,),
        in_specs=_halo_specs(tm, d, t) + [
            _const_spec(wup.shape), _const_spec(wc.shape), _const_spec(wdn.shape),
            _const_spec(g.shape), _const_spec(b.shape),
        ],
        out_specs=pl.BlockSpec((tm, d), lambda i: (i, 0)),
        out_shape=jax.ShapeDtypeStruct((t, d), _F32),
        scratch_shapes=[
            pltpu.VMEM((tm + 2 * _SUBLANES, d), _F32),
            pltpu.VMEM((tm + 2 * _SUBLANES, 2 * dff), _F32),
        ],
        compiler_params=_params("parallel"),
        name="conv_ffn",
    )(x2d, x2d, x2d, wup, wc, wdn, g, b)


def _hyena_in_kernel(xp_ref, x_ref, xn_ref, win_ref, wc_ref, v_ref, g1_ref, g2_ref,
                     xcat_ref, h_ref, *, tm, tiles_per_seq, d):
    _assemble_halo(xp_ref, x_ref, xn_ref, xcat_ref, tm, tiles_per_seq)
    h_ref[...] = _dot(xcat_ref[...].astype(_BF16), win_ref[...])
    u = _dwconv3(h_ref, wc_ref, tm)
    v_ref[...] = u[:, :d]
    g1_ref[...] = u[:, d:2 * d]
    g2_ref[...] = u[:, 2 * d:]


def _hyena_in(x2d, win, wc, seq, tm):
    t, d = x2d.shape
    out = jax.ShapeDtypeStruct((t, d), _F32)
    ospec = pl.BlockSpec((tm, d), lambda i: (i, 0))
    return pl.pallas_call(
        functools.partial(_hyena_in_kernel, tm=tm, tiles_per_seq=seq // tm, d=d),
        grid=(t // tm,),
        in_specs=_halo_specs(tm, d, t) + [_const_spec(win.shape), _const_spec(wc.shape)],
        out_specs=[ospec, ospec, ospec],
        out_shape=[out, out, out],
        scratch_shapes=[
            pltpu.VMEM((tm + 2 * _SUBLANES, d), _F32),
            pltpu.VMEM((tm + 2 * _SUBLANES, 3 * d), _F32),
        ],
        compiler_params=_params("parallel"),
        name="hyena_in",
    )(x2d, x2d, x2d, win, wc)


def _filter_kernel(fl_ref, w1_ref, b1_ref, w2_ref, b2_ref, w3_ref, b3_ref, fr_ref, wo_ref, wob_ref,
                   dl_ref, k_ref, *, rows, length):
    i = pl.program_id(0)
    n = 2 * length
    r = i * rows + lax.broadcasted_iota(jnp.int32, (rows, 1), 0)
    lag = jnp.where(r < length, r, n - r).astype(_F32)
    tpos = lag / (length - 1.0)
    wang = lag * (2.0 * math.pi / length)
    lane = lax.broadcasted_iota(jnp.int32, (rows, _LANES), 1)
    a = wang * fl_ref[...]
    z = jnp.where(lane == 0, tpos,
                  jnp.where(lane <= _HY_BANDS, jnp.cos(a),
                            jnp.where(lane <= 2 * _HY_BANDS, -jnp.sin(a), 0.0)))
    fr = fr_ref[...]
    h = jnp.sin(fr * (_dot(z.astype(_BF16), w1_ref[...]) + b1_ref[...]))
    h = jnp.sin(fr * (_dot(h.astype(_BF16), w2_ref[...]) + b2_ref[...]))
    h = jnp.sin(fr * (_dot(h.astype(_BF16), w3_ref[...]) + b3_ref[...]))
    hb = h.astype(_BF16)
    out = _dot(hb, wo_ref[...])
    extra = _dot(hb[0:_SUBLANES], wob_ref[...])
    decay = jnp.exp(-tpos * dl_ref[...])
    d = dl_ref.shape[1]
    keep = r != length
    for o in range(2):
        ko = jnp.where(keep, out[:, o * d:(o + 1) * d] * decay, 0.0)
        row0 = extra[0:1, o * d:(o + 1) * d] * decay[0:1, :]
        k_ref[o] = (ko + jnp.where(r == 0, row0, 0.0)).astype(k_ref.dtype)


def _hyena_kernel_rows(fl, w1, b1, w2, b2, w3, b3, fr, wo2, dl, length, rows):
    n = 2 * length
    d = dl.shape[1]
    half = length // rows
    return pl.pallas_call(
        functools.partial(_filter_kernel, rows=rows, length=length),
        grid=(n // rows,),
        in_specs=[
            _const_spec(fl.shape), _const_spec(w1.shape), _const_spec(b1.shape),
            _const_spec(w2.shape), _const_spec(b2.shape), _const_spec(w3.shape), _const_spec(b3.shape),
            _const_spec(fr.shape),
            pl.BlockSpec((None,) + wo2.shape[1:], lambda i: (i // half, 0, 0)),
            pl.BlockSpec((None,) + wo2.shape[1:], lambda i: (1, 0, 0)),
            _const_spec(dl.shape),
        ],
        out_specs=pl.BlockSpec((2, rows, d), lambda i: (0, i, 0)),
        out_shape=jax.ShapeDtypeStruct((2, n, d), _BF16),
        compiler_params=_params("parallel"),
        name="hyena_filter",
    )(fl, w1, b1, w2, b2, w3, b3, fr, wo2, wo2, dl)


def _dft_consts(n1):
    n = n1 * _N2
    k = np.arange(n1)
    ang1 = -2.0 * np.pi * ((k[:, None] * k[None, :]) % n1) / n1
    f1r, f1i = np.cos(ang1), np.sin(ang1)
    half = n1 // 2
    fwd_c = np.block([[f1r[:, :half], -f1i[:, :half]], [f1i[:, :half], f1r[:, :half]]])
    fwd_r = np.concatenate([f1r, f1i], axis=0)
    inv_c = np.block([[f1r[:half], f1i[:half]], [-f1i[:half], f1r[:half]]]) / n
    k2 = np.arange(_N2)
    ang2 = -2.0 * np.pi * ((k2[:, None] * k2[None, :]) % _N2) / _N2
    angt = -2.0 * np.pi * ((k[:, None] * k2[None, :]) % n) / n
    f32 = lambda a: jnp.asarray(a.astype(np.float32))
    return dict(fwd_c=f32(fwd_c), fwd_r=f32(fwd_r), inv_c=f32(inv_c),
                f2r=f32(np.cos(ang2)), f2i=f32(np.sin(ang2)),
                twr=f32(np.cos(angt)), twi=f32(np.sin(angt)))


def _outer_fwd_kernel(m_ref, x_ref, o_ref):
    xs = x_ref[...]
    xb = xs.reshape(xs.shape[0] * xs.shape[1], xs.shape[2]).astype(_BF16)
    o_ref[...] = _dot(m_ref[...], xb).astype(o_ref.dtype)


def _outer_fwd(m, x3, wc):
    p, r, c = x3.shape
    rows = m.shape[0]
    return pl.pallas_call(
        _outer_fwd_kernel,
        grid=(c // wc,),
        in_specs=[_const_spec(m.shape), pl.BlockSpec((p, r, wc), lambda j: (0, 0, j))],
        out_specs=pl.BlockSpec((rows, wc), lambda j: (0, j)),
        out_shape=jax.ShapeDtypeStruct((rows, c), _BF16),
        compiler_params=_params("parallel"),
        name="fft_outer_fwd",
    )(m, x3)


def _mid_kernel(a_ref, ak_ref, f2r_ref, f2i_ref, twr_ref, twi_ref, b_ref, *, kb):
    i = pl.program_id(0)
    f2r = f2r_ref[...]
    f2i = f2i_ref[...]
    for j in range(kb):
        k1 = i * kb + j
        tr = twr_ref[pl.ds(k1, 1), :]
        ti = twi_ref[pl.ds(k1, 1), :]
        cr = f2r * tr - f2i * ti
        ci = f2r * ti + f2i * tr
        m2 = jnp.concatenate([jnp.concatenate([cr, -ci], axis=1),
                              jnp.concatenate([ci, cr], axis=1)], axis=0).astype(_BF16)
        a = a_ref[:, j].reshape(2 * _N2, a_ref.shape[-1])
        ak = ak_ref[:, j].reshape(2 * _N2, ak_ref.shape[-1])
        x = _dot(m2, a)
        kf = _dot(m2, ak)
        xr, xi = x[:_N2], x[_N2:]
        kr, ki = kf[:_N2], kf[_N2:]
        y = jnp.concatenate([xr * kr - xi * ki, xr * ki + xi * kr], axis=0).astype(_BF16)
        bt = lax.dot_general(m2, y, (((0,), (0,)), ((), ())), preferred_element_type=_F32)
        b_ref[:, j] = bt.reshape(2, _N2, bt.shape[-1]).astype(b_ref.dtype)


def _mid(a4, ak4, c, kb):
    _, n1, n2, d = a4.shape
    blk = pl.BlockSpec((2, kb, n2, d), lambda i: (0, i, 0, 0))
    return pl.pallas_call(
        functools.partial(_mid_kernel, kb=kb),
        grid=(n1 // kb,),
        in_specs=[blk, blk, _const_spec(c["f2r"].shape), _const_spec(c["f2i"].shape),
                  _const_spec(c["twr"].shape), _const_spec(c["twi"].shape)],
        out_specs=blk,
        out_shape=jax.ShapeDtypeStruct(a4.shape, _BF16),
        compiler_params=_params("parallel"),
        name="fft_mid",
    )(a4, ak4, c["f2r"], c["f2i"], c["twr"], c["twi"])


def _outer_inv_kernel(minv_ref, mfwd_ref, b_ref, z_ref, gate_ref, bias_ref, zo_ref, ao_ref):
    y = _dot(minv_ref[...], b_ref[...])
    half = y.shape[0] // 2
    y3 = y.reshape(2, half, y.shape[1])
    zn = gate_ref[...] * (y3 + bias_ref[...] * z_ref[...])
    zo_ref[...] = zn
    if ao_ref is not None:
        ao_ref[...] = _dot(mfwd_ref[...], zn.reshape(y.shape).astype(_BF16)).astype(ao_ref.dtype)


def _outer_inv_last_kernel(minv_ref, b_ref, z_ref, gate_ref, bias_ref, zo_ref):
    _outer_inv_kernel(minv_ref, None, b_ref, z_ref, gate_ref, bias_ref, zo_ref, None)


def _outer_inv(minv, mfwd, b2, z3, gate3, bias_t, wc):
    p, r, c = z3.shape
    zspec = pl.BlockSpec((p, r, wc), lambda j: (0, 0, j))
    zshape = jax.ShapeDtypeStruct(z3.shape, _F32)
    common = [pl.BlockSpec((b2.shape[0], wc), lambda j: (0, j)), zspec, zspec,
              pl.BlockSpec((1, wc), lambda j: (0, 0), pipeline_mode=pl.Buffered(1))]
    if mfwd is None:
        return pl.pallas_call(
            _outer_inv_last_kernel,
            grid=(c // wc,),
            in_specs=[_const_spec(minv.shape)] + common,
            out_specs=zspec,
            out_shape=zshape,
            compiler_params=_params("parallel"),
            name="fft_outer_inv_last",
        )(minv, b2, z3, gate3, bias_t)
    rows = mfwd.shape[0]
    return pl.pallas_call(
        _outer_inv_kernel,
        grid=(c // wc,),
        in_specs=[_const_spec(minv.shape), _const_spec(mfwd.shape)] + common,
        out_specs=[zspec, pl.BlockSpec((rows, wc), lambda j: (0, j))],
        out_shape=[zshape, jax.ShapeDtypeStruct((rows, c), _BF16)],
        compiler_params=_params("parallel"),
        name="fft_outer_inv_fwd",
    )(minv, mfwd, b2, z3, gate3, bias_t)


def _pick(n, target):
    t = min(n, target)
    while n % t:
        t //= 2
    return t


def _hyena_mixer(x2d, batch, seq, w_in, w_short, fw1, fb1, fw2, fb2, fw3, fb3, freq, fw_out, d_bias):
    assert batch == 2, "batch 0 / batch 1 are packed as real / imaginary parts of one FFT"
    t, d = x2d.shape
    n = 2 * seq
    n1 = n // _N2
    half = n1 // 2
    hf = fw1.shape[1]
    tm = _pick(seq, 256)
    v, g1, g2 = _hyena_in(x2d, w_in.astype(_BF16), w_short, seq, tm)

    fl = jnp.concatenate([jnp.zeros((1,), _F32),
                          jnp.linspace(1e-4, _HY_BANDS - 1, _HY_BANDS, dtype=_F32),
                          jnp.linspace(1e-4, _HY_BANDS - 1, _HY_BANDS, dtype=_F32),
                          jnp.zeros((_LANES - _HY_EMB,), _F32)])[None, :]
    w1p = jnp.zeros((_LANES, hf), _F32).at[:_HY_EMB].set(fw1).astype(_BF16)
    deltas = jnp.abs(jnp.linspace(math.log(_HY_FAST) / _HY_TARGET, math.log(_HY_SLOW) / _HY_TARGET,
                                  d, dtype=_F32))[None, :]
    wo4 = fw_out.reshape(hf, 2, 2, d)
    wo2 = jnp.transpose(wo4, (2, 0, 1, 3)).reshape(2, hf, 2 * d).astype(_BF16)
    rows = _pick(seq, 512)
    kern = _hyena_kernel_rows(fl, w1p, fb1[None, :], fw2.astype(_BF16), fb2[None, :],
                              fw3.astype(_BF16), fb3[None, :], freq[None, :], wo2, deltas, seq, rows)

    c = _dft_consts(n1)
    cols = _N2 * d
    wc = _pick(cols, 2048)
    kb = _pick(n1, 4)
    fwd_c = c["fwd_c"].astype(_BF16)
    fwd_r = c["fwd_r"].astype(_BF16)
    inv_c = c["inv_c"].astype(_BF16)

    z3 = v.reshape(batch, half, cols)
    gates = (g1.reshape(batch, half, cols), g2.reshape(batch, half, cols))
    a = _outer_fwd(fwd_c, z3, wc)
    for o in range(2):
        ak = _outer_fwd(fwd_r, kern[o].reshape(1, n1, cols), wc)
        bsp = _mid(a.reshape(2, n1, _N2, d), ak.reshape(2, n1, _N2, d), c, kb)
        bias_t = jnp.tile(d_bias[o][None, :], (1, wc // d))
        if o == 0:
            z3, a = _outer_inv(inv_c, fwd_c, bsp.reshape(2 * n1, cols), z3, gates[o], bias_t, wc)
        else:
            z3 = _outer_inv(inv_c, None, bsp.reshape(2 * n1, cols), z3, gates[o], bias_t, wc)
    return z3.reshape(t, d)


def _mla_weights(w_in, w_uq, w_ukv):
    d = w_in.shape[0]
    r0 = _Q_LORA + _KV_LORA
    hr = _ROPE // 2
    zpad = jnp.zeros((d, 64), w_in.dtype)
    win = jnp.concatenate([w_in, zpad, w_in[:, r0 + hr:r0 + _ROPE], w_in[:, r0:r0 + hr], zpad], axis=1)
    wq3 = w_uq.reshape(_Q_LORA, _HEADS, _QK)
    wq_n = wq3[:, :, :_NOPE].reshape(_Q_LORA, _HEADS * _NOPE)
    wq_r = wq3[:, :, _NOPE:].reshape(_Q_LORA, _HEADS * _ROPE)
    wq_s = jnp.concatenate([wq3[:, :, _NOPE + hr:], wq3[:, :, _NOPE:_NOPE + hr]], axis=2)
    wq = jnp.concatenate([wq_n, wq_r, wq_s.reshape(_Q_LORA, _HEADS * _ROPE)], axis=1)
    wkv3 = w_ukv.reshape(_KV_LORA, _HEADS, _NOPE + _VDIM)
    wkv = jnp.concatenate([wkv3[:, :, :_NOPE].reshape(_KV_LORA, _HEADS * _NOPE),
                           wkv3[:, :, _NOPE:].reshape(_KV_LORA, _HEADS * _VDIM)], axis=1)
    return win.astype(_BF16), wq.astype(_BF16), wkv.astype(_BF16)


def _mla_mixer(x2d, positions, batch, seq, w_in, g_q, w_uq, g_kv, w_ukv):
    win, wq, wkv = _mla_weights(w_in, w_uq, w_ukv)
    inv = 1.0 / (_ROPE_THETA ** (jnp.arange(0, _ROPE, 2, dtype=_F32) / _ROPE))
    invf = jnp.tile(inv, 4)[None, :]
    sgn = jnp.tile(jnp.concatenate([-jnp.ones((_ROPE // 2,), _F32), jnp.ones((_ROPE // 2,), _F32)]), 2)[None, :]
    tm = _pick(seq, 512)
    q, k, v = _mla_proj(x2d, positions.reshape(-1, 1), win, g_q[None, :], g_kv[None, :], wq, wkv,
                        invf, sgn, batch, seq, tm)
    o = _attention(q, k, v, _pick(seq, 1024), _pick(seq, 1024))
    return o.reshape(batch * seq, _HEADS * _VDIM)


def kernel(x, positions, mla_w_in, mla_g_q, mla_w_uq, mla_g_kv, mla_w_ukv, mla_w_o, hy_w_in, hy_w_short, hy_fw1, hy_fb1, hy_fw2, hy_fb2, hy_fw3, hy_fb3, hy_freq, hy_fw_out, hy_d_bias, hy_w_o, ffn_w_up, ffn_w_conv, ffn_w_down, ln1_g, ln1_b, ln2_g, ln2_b):
    batch, seq, d = x.shape
    x2d = x.reshape(batch * seq, d)
    tm = _pick(seq, 256)
    for i in range(_DEPTH):
        j = i // 2
        if i % 2 == 0:
            a = _mla_mixer(x2d, positions, batch, seq, mla_w_in[j], mla_g_q[j], mla_w_uq[j],
                           mla_g_kv[j], mla_w_ukv[j])
            w_o = mla_w_o[j]
        else:
            a = _hyena_mixer(x2d, batch, seq, hy_w_in[j], hy_w_short[j], hy_fw1[j], hy_fb1[j], hy_fw2[j],
                             hy_fb2[j], hy_fw3[j], hy_fb3[j], hy_freq[j], hy_fw_out[j], hy_d_bias[j])
            w_o = hy_w_o[j]
        x2d = _proj_ln(a, x2d, w_o.astype(_BF16), ln1_g[i][None, :], ln1_b[i][None, :], tm)
        x2d = _conv_ffn(x2d, ffn_w_up[i].astype(_BF16), ffn_w_conv[i], ffn_w_down[i].astype(_BF16),
                        ln2_g[i][None, :], ln2_b[i][None, :], seq, tm)
    return x2d.reshape(batch, seq, d)
```

```python
import functools
import math

import numpy as np
import jax
import jax.numpy as jnp
from jax import lax
from jax.experimental import pallas as pl
from jax.experimental.pallas import tpu as pltpu

_F32 = jnp.float32
_BF16 = jnp.bfloat16

_HEADS = 8
_NOPE = 128
_ROPE = 64
_VDIM = 128
_QK = _NOPE + _ROPE
_Q_LORA = 384
_KV_LORA = 256
_ROPE_THETA = 10000.0
_HY_EMB = 33
_HY_BANDS = (_HY_EMB - 1) // 2
_HY_FAST, _HY_SLOW, _HY_TARGET = 0.3, 1.5, 1e-2
_NORM_EPS = 1e-5
_RMS_EPS = 1e-6
_DEPTH = 2
_ALPHA = (2.0 * _DEPTH) ** 0.25

_LANES = 128
_SUBLANES = 8
_VMEM_LIMIT = 56 * 1024 * 1024

_N2 = 128


def _params(*sem):
    return pltpu.CompilerParams(dimension_semantics=sem, vmem_limit_bytes=_VMEM_LIMIT)


def _const_spec(shape):
    nd = len(shape)
    return pl.BlockSpec(shape, lambda *_: (0,) * nd, pipeline_mode=pl.Buffered(1))


def _dot(a, b):
    return jnp.dot(a, b, preferred_element_type=_F32)


def _layer_norm(y, g, b):
    mu = jnp.mean(y, axis=-1, keepdims=True)
    yc = y - mu
    var = jnp.mean(yc * yc, axis=-1, keepdims=True)
    return yc * lax.rsqrt(var + _NORM_EPS) * g + b


def _mla_proj_kernel(x_ref, posc_ref, posr_ref, win_ref, gq_ref, gkv_ref, wqt_ref, wk_ref, wvt_ref,
                     invf_ref, invb_ref, qt_ref, k_ref, vt_ref, *, qscale):
    xb = x_ref[...].astype(_BF16)
    h = _dot(xb, win_ref[...])
    cq = h[:, :_Q_LORA]
    ckv = h[:, _Q_LORA:_Q_LORA + _KV_LORA]
    kr = h[:, 640:704]
    krs = h[:, 768:832]

    def rms(c, g):
        ms = jnp.mean(c * c, axis=-1, keepdims=True)
        return (c * lax.rsqrt(ms + _RMS_EPS) * g).astype(_BF16)

    cqn = rms(cq, gq_ref[...])
    ckvn = rms(ckv, gkv_ref[...])
    nt = (((1,), (1,)), ((), ()))
    qt = lax.dot_general(wqt_ref[...], cqn, nt, preferred_element_type=_F32)
    vt = lax.dot_general(wvt_ref[...], ckvn, nt, preferred_element_type=_F32)
    kn = _dot(ckvn, wk_ref[...])

    ang = posc_ref[...].astype(_F32) * invf_ref[...]
    lane = lax.broadcasted_iota(jnp.int32, (1, _LANES), 1)
    sgn = jnp.where(lane % _ROPE < _ROPE // 2, -1.0, 1.0)
    k_rope = (kr * jnp.cos(ang)[:, :_ROPE] + krs * (jnp.sin(ang) * sgn)[:, :_ROPE]).astype(_BF16)

    angt = invb_ref[...] * posr_ref[...].astype(_F32)
    row = lax.broadcasted_iota(jnp.int32, (_ROPE, 1), 0)
    ct = jnp.cos(angt)
    st = jnp.sin(angt) * jnp.where(row < _ROPE // 2, -1.0, 1.0)
    nh = _HEADS * _NOPE
    nr = _HEADS * _ROPE
    for hd in range(_HEADS):
        qt_ref[0, hd, :_NOPE, :] = (qt[hd * _NOPE:(hd + 1) * _NOPE] * qscale).astype(_BF16)
        r = qt[nh + hd * _ROPE:nh + (hd + 1) * _ROPE]
        rs = qt[nh + nr + hd * _ROPE:nh + nr + (hd + 1) * _ROPE]
        qt_ref[0, hd, _NOPE:, :] = ((r * ct + rs * st) * qscale).astype(_BF16)
        k_ref[0, hd, :, :_NOPE] = kn[:, hd * _NOPE:(hd + 1) * _NOPE].astype(_BF16)
        k_ref[0, hd, :, _NOPE:] = k_rope
        vt_ref[0, hd] = vt[hd * _VDIM:(hd + 1) * _VDIM].astype(_BF16)


def _mla_proj(x2d, posc, posr, win, gq, gkv, wqt, wk, wvt, invf, invb, batch, seq, tm):
    t, d = x2d.shape
    spb = seq // tm
    qscale = (_QK ** -0.5) * math.log2(math.e)
    consts = [win, gq, gkv, wqt, wk, wvt, invf, invb]
    return pl.pallas_call(
        functools.partial(_mla_proj_kernel, qscale=qscale),
        grid=(t // tm,),
        in_specs=[
            pl.BlockSpec((tm, d), lambda i: (i, 0)),
            pl.BlockSpec((tm, 1), lambda i: (i, 0)),
            pl.BlockSpec((1, tm), lambda i: (0, i)),
        ] + [_const_spec(c.shape) for c in consts],
        out_specs=[
            pl.BlockSpec((1, _HEADS, _QK, tm), lambda i: (i // spb, 0, 0, i % spb)),
            pl.BlockSpec((1, _HEADS, tm, _QK), lambda i: (i // spb, 0, i % spb, 0)),
            pl.BlockSpec((1, _HEADS, _VDIM, tm), lambda i: (i // spb, 0, 0, i % spb)),
        ],
        out_shape=[
            jax.ShapeDtypeStruct((batch, _HEADS, _QK, seq), _BF16),
            jax.ShapeDtypeStruct((batch, _HEADS, seq, _QK), _BF16),
            jax.ShapeDtypeStruct((batch, _HEADS, _VDIM, seq), _BF16),
        ],
        compiler_params=_params("parallel"),
        name="mla_proj",
    )(x2d, posc, posr, *consts)


_ONES_ROWS = 16


def _attn_kernel(qt_ref, k_ref, vt_ref, o_ref, m_ref, acc_ref, *, sub, nsub):
    ki = pl.program_id(3)

    @pl.when(ki == 0)
    def _():
        m_ref[...] = jnp.full(m_ref.shape, -jnp.inf, _F32)
        acc_ref[...] = jnp.zeros(acc_ref.shape, _F32)

    qt = qt_ref[0, 0]
    ones = (lax.broadcasted_iota(jnp.int32, (_ONES_ROWS, sub), 0) == 0).astype(_BF16)
    m = m_ref[...]
    st_next = _dot(k_ref[0, 0, 0:sub, :], qt)
    for j in range(nsub):
        st = st_next
        if j + 1 < nsub:
            st_next = _dot(k_ref[0, 0, (j + 1) * sub:(j + 2) * sub, :], qt)
        m_new = jnp.maximum(m, jnp.max(st, axis=0, keepdims=True))
        alpha = jnp.exp2(m - m_new)
        p = jnp.exp2(st - m_new).astype(_BF16)
        vte = jnp.concatenate([vt_ref[0, 0, :, j * sub:(j + 1) * sub], ones], axis=0)
        acc_ref[...] = alpha * acc_ref[...] + _dot(vte, p)
        m = m_new
    m_ref[...] = m

    @pl.when(ki == pl.num_programs(3) - 1)
    def _():
        acc = acc_ref[...]
        o = acc[:_VDIM] / acc[_VDIM:_VDIM + 1]
        o_ref[0] = o.T.astype(o_ref.dtype)


def _attention(qt, k, vt, tq, tk, sub):
    b, h, s, _ = k.shape
    return pl.pallas_call(
        functools.partial(_attn_kernel, sub=sub, nsub=tk // sub),
        grid=(b, h, s // tq, s // tk),
        in_specs=[
            pl.BlockSpec((1, 1, _QK, tq), lambda bi, hi, qi, ki: (bi, hi, 0, qi)),
            pl.BlockSpec((1, 1, tk, _QK), lambda bi, hi, qi, ki: (bi, hi, ki, 0)),
            pl.BlockSpec((1, 1, _VDIM, tk), lambda bi, hi, qi, ki: (bi, hi, 0, ki)),
        ],
        out_specs=pl.BlockSpec((1, tq, _VDIM), lambda bi, hi, qi, ki: (bi, qi, hi)),
        out_shape=jax.ShapeDtypeStruct((b, s, h * _VDIM), _BF16),
        scratch_shapes=[
            pltpu.VMEM((1, tq), _F32),
            pltpu.VMEM((_VDIM + _ONES_ROWS, tq), _F32),
        ],
        compiler_params=_params("parallel", "parallel", "parallel", "arbitrary"),
        name="mla_attention",
    )(qt, k, vt)


def _proj_ln_kernel(a_ref, x_ref, w_ref, g_ref, b_ref, o_ref):
    m = _dot(a_ref[...].astype(_BF16), w_ref[...])
    o_ref[...] = _layer_norm(_ALPHA * x_ref[...] + m, g_ref[...], b_ref[...])


def _proj_ln(a2d, x2d, w, g, b, tm):
    t, d = x2d.shape
    ka = a2d.shape[1]
    return pl.pallas_call(
        _proj_ln_kernel,
        grid=(t // tm,),
        in_specs=[
            pl.BlockSpec((tm, ka), lambda i: (i, 0)),
            pl.BlockSpec((tm, d), lambda i: (i, 0)),
            _const_spec(w.shape), _const_spec(g.shape), _const_spec(b.shape),
        ],
        out_specs=pl.BlockSpec((tm, d), lambda i: (i, 0)),
        out_shape=jax.ShapeDtypeStruct((t, d), _F32),
        compiler_params=_params("parallel"),
        name="proj_ln",
    )(a2d, x2d, w, g, b)


def _halo_specs(tm, d, nrows):
    r = tm // _SUBLANES
    last = nrows // _SUBLANES - 1
    return [
        pl.BlockSpec((_SUBLANES, d), lambda i: (jnp.maximum(i * r - 1, 0), 0)),
        pl.BlockSpec((tm, d), lambda i: (i, 0)),
        pl.BlockSpec((_SUBLANES, d), lambda i: (jnp.minimum((i + 1) * r, last), 0)),
    ]


def _assemble_halo(xp_ref, x_ref, xn_ref, xcat_ref, tm, tiles_per_seq):
    i = pl.program_id(0)
    first = (i % tiles_per_seq) == 0
    last = (i % tiles_per_seq) == tiles_per_seq - 1
    xcat_ref[0:_SUBLANES, :] = jnp.where(first, 0.0, xp_ref[...])
    xcat_ref[_SUBLANES:_SUBLANES + tm, :] = x_ref[...]
    xcat_ref[_SUBLANES + tm:, :] = jnp.where(last, 0.0, xn_ref[...])


def _dwconv3(h_ref, wc_ref, tm):
    lo = h_ref[pl.ds(_SUBLANES - 1, tm), :]
    mid = h_ref[pl.ds(_SUBLANES, tm), :]
    hi = h_ref[pl.ds(_SUBLANES + 1, tm), :]
    return lo * wc_ref[0:1, :] + mid * wc_ref[1:2, :] + hi * wc_ref[2:3, :]


def _conv_ffn_kernel(xp_ref, x_ref, xn_ref, wup_ref, wc_ref, wdn_ref, g_ref, b_ref, o_ref,
                     xcat_ref, h_ref, *, tm, tiles_per_seq, dff):
    _assemble_halo(xp_ref, x_ref, xn_ref, xcat_ref, tm, tiles_per_seq)
    h_ref[...] = _dot(xcat_ref[...].astype(_BF16), wup_ref[...])
    hc = _dwconv3(h_ref, wc_ref, tm)
    a = hc[:, :dff]
    gt = hc[:, dff:]
    act = (gt * jax.nn.sigmoid(gt) * a).astype(_BF16)
    f = _dot(act, wdn_ref[...])
    o_ref[...] = _layer_norm(_ALPHA * x_ref[...] + f, g_ref[...], b_ref[...])


def _conv_ffn(x2d, wup, wc, wdn, g, b, seq, tm):
    t, d = x2d.shape
    dff = wdn.shape[0]
    return pl.pallas_call(
        functools.partial(_conv_ffn_kernel, tm=tm, tiles_per_seq=seq // tm, dff=dff),
        grid=(t // tm,),
        in_specs=_halo_specs(tm, d, t) + [
            _const_spec(wup.shape), _const_spec(wc.shape), _const_spec(wdn.shape),
            _const_spec(g.shape), _const_spec(b.shape),
        ],
        out_specs=pl.BlockSpec((tm, d), lambda i: (i, 0)),
        out_shape=jax.ShapeDtypeStruct((t, d), _F32),
        scratch_shapes=[
            pltpu.VMEM((tm + 2 * _SUBLANES, d), _F32),
            pltpu.VMEM((tm + 2 * _SUBLANES, 2 * dff), _F32),
        ],
        compiler_params=_params("parallel"),
        name="conv_ffn",
    )(x2d, x2d, x2d, wup, wc, wdn, g, b)


def _hyena_in_kernel(xp_ref, x_ref, xn_ref, win_ref, wc_ref, v_ref, g1_ref, g2_ref,
                     xcat_ref, h_ref, *, tm, tiles_per_seq, d):
    _assemble_halo(xp_ref, x_ref, xn_ref, xcat_ref, tm, tiles_per_seq)
    h_ref[...] = _dot(xcat_ref[...].astype(_BF16), win_ref[...])
    u = _dwconv3(h_ref, wc_ref, tm)
    v_ref[...] = u[:, :d]
    g1_ref[...] = u[:, d:2 * d]
    g2_ref[...] = u[:, 2 * d:]


def _hyena_in(x2d, win, wc, seq, tm):
    t, d = x2d.shape
    out = jax.ShapeDtypeStruct((t, d), _F32)
    ospec = pl.BlockSpec((tm, d), lambda i: (i, 0))
    return pl.pallas_call(
        functools.partial(_hyena_in_kernel, tm=tm, tiles_per_seq=seq // tm, d=d),
        grid=(t // tm,),
        in_specs=_halo_specs(tm, d, t) + [_const_spec(win.shape), _const_spec(wc.shape)],
        out_specs=[ospec, ospec, ospec],
        out_shape=[out, out, out],
        scratch_shapes=[
            pltpu.VMEM((tm + 2 * _SUBLANES, d), _F32),
            pltpu.VMEM((tm + 2 * _SUBLANES, 3 * d), _F32),
        ],
        compiler_params=_params("parallel"),
        name="hyena_in",
    )(x2d, x2d, x2d, win, wc)


def _filter_kernel(fl_ref, w1_ref, b1_ref, w2_ref, b2_ref, w3_ref, b3_ref, fr_ref, wo_ref, wob_ref,
                   dl_ref, k_ref, *, rows, length):
    i = pl.program_id(0)
    n = 2 * length
    r = i * rows + lax.broadcasted_iota(jnp.int32, (rows, 1), 0)
    lag = jnp.where(r < length, r, n - r).astype(_F32)
    tpos = lag / (length - 1.0)
    wang = lag * (2.0 * math.pi / length)
    lane = lax.broadcasted_iota(jnp.int32, (rows, _LANES), 1)
    a = wang * fl_ref[...]
    z = jnp.where(lane == 0, tpos,
                  jnp.where(lane <= _HY_BANDS, jnp.cos(a),
                            jnp.where(lane <= 2 * _HY_BANDS, -jnp.sin(a), 0.0)))
    fr = fr_ref[...]
    h = jnp.sin(fr * (_dot(z.astype(_BF16), w1_ref[...]) + b1_ref[...]))
    h = jnp.sin(fr * (_dot(h.astype(_BF16), w2_ref[...]) + b2_ref[...]))
    h = jnp.sin(fr * (_dot(h.astype(_BF16), w3_ref[...]) + b3_ref[...]))
    hb = h.astype(_BF16)
    out = _dot(hb, wo_ref[...])
    extra = _dot(hb[0:_SUBLANES], wob_ref[...])
    decay = jnp.exp(-tpos * dl_ref[...])
    d = dl_ref.shape[1]
    keep = r != length
    for o in range(2):
        ko = jnp.where(keep, out[:, o * d:(o + 1) * d] * decay, 0.0)
        row0 = extra[0:1, o * d:(o + 1) * d] * decay[0:1, :]
        k_ref[o] = (ko + jnp.where(r == 0, row0, 0.0)).astype(k_ref.dtype)


def _hyena_kernel_rows(fl, w1, b1, w2, b2, w3, b3, fr, wo2, dl, length, rows):
    n = 2 * length
    d = dl.shape[1]
    half = length // rows
    return pl.pallas_call(
        functools.partial(_filter_kernel, rows=rows, length=length),
        grid=(n // rows,),
        in_specs=[
            _const_spec(fl.shape), _const_spec(w1.shape), _const_spec(b1.shape),
            _const_spec(w2.shape), _const_spec(b2.shape), _const_spec(w3.shape), _const_spec(b3.shape),
            _const_spec(fr.shape),
            pl.BlockSpec((None,) + wo2.shape[1:], lambda i: (i // half, 0, 0)),
            pl.BlockSpec((None,) + wo2.shape[1:], lambda i: (1, 0, 0)),
            _const_spec(dl.shape),
        ],
        out_specs=pl.BlockSpec((2, rows, d), lambda i: (0, i, 0)),
        out_shape=jax.ShapeDtypeStruct((2, n, d), _BF16),
        compiler_params=_params("parallel"),
        name="hyena_filter",
    )(fl, w1, b1, w2, b2, w3, b3, fr, wo2, wo2, dl)


def _dft_consts(n1):
    n = n1 * _N2
    k = np.arange(n1)
    ang1 = -2.0 * np.pi * ((k[:, None] * k[None, :]) % n1) / n1
    f1r, f1i = np.cos(ang1), np.sin(ang1)
    half = n1 // 2
    fwd_c = np.block([[f1r[:, :half], -f1i[:, :half]], [f1i[:, :half], f1r[:, :half]]])
    fwd_r = np.concatenate([f1r, f1i], axis=0)
    inv_c = np.block([[f1r[:half], f1i[:half]], [-f1i[:half], f1r[:half]]]) / n
    k2 = np.arange(_N2)
    ang2 = -2.0 * np.pi * ((k2[:, None] * k2[None, :]) % _N2) / _N2
    angt = -2.0 * np.pi * ((k[:, None] * k2[None, :]) % n) / n
    f32 = lambda a: jnp.asarray(a.astype(np.float32))
    return dict(fwd_c=f32(fwd_c), fwd_r=f32(fwd_r), inv_c=f32(inv_c),
                f2r=f32(np.cos(ang2)), f2i=f32(np.sin(ang2)),
                twr=f32(np.cos(angt)), twi=f32(np.sin(angt)))


def _outer_fwd_kernel(m_ref, x_ref, o_ref):
    xs = x_ref[...]
    xb = xs.reshape(xs.shape[0] * xs.shape[1], xs.shape[2]).astype(_BF16)
    o_ref[...] = _dot(m_ref[...], xb).astype(o_ref.dtype)


def _outer_fwd(m, x3, wc):
    p, r, c = x3.shape
    rows = m.shape[0]
    return pl.pallas_call(
        _outer_fwd_kernel,
        grid=(c // wc,),
        in_specs=[_const_spec(m.shape), pl.BlockSpec((p, r, wc), lambda j: (0, 0, j))],
        out_specs=pl.BlockSpec((rows, wc), lambda j: (0, j)),
        out_shape=jax.ShapeDtypeStruct((rows, c), _BF16),
        compiler_params=_params("parallel"),
        name="fft_outer_fwd",
    )(m, x3)


def _mid_kernel(a_ref, ak_ref, f2r_ref, f2i_ref, twr_ref, twi_ref, b_ref, *, kb):
    i = pl.program_id(0)
    f2r = f2r_ref[...]
    f2i = f2i_ref[...]
    for j in range(kb):
        k1 = i * kb + j
        tr = twr_ref[pl.ds(k1, 1), :]
        ti = twi_ref[pl.ds(k1, 1), :]
        cr = f2r * tr - f2i * ti
        ci = f2r * ti + f2i * tr
        m2 = jnp.concatenate([jnp.concatenate([cr, -ci], axis=1),
                              jnp.concatenate([ci, cr], axis=1)], axis=0).astype(_BF16)
        a = a_ref[:, j].reshape(2 * _N2, a_ref.shape[-1])
        ak = ak_ref[:, j].reshape(2 * _N2, ak_ref.shape[-1])
        x = _dot(m2, a)
        kf = _dot(m2, ak)
        xr, xi = x[:_N2], x[_N2:]
        kr, ki = kf[:_N2], kf[_N2:]
        y = jnp.concatenate([xr * kr - xi * ki, xr * ki + xi * kr], axis=0).astype(_BF16)
        bt = lax.dot_general(m2, y, (((0,), (0,)), ((), ())), preferred_element_type=_F32)
        b_ref[:, j] = bt.reshape(2, _N2, bt.shape[-1]).astype(b_ref.dtype)


def _mid(a4, ak4, c, kb):
    _, n1, n2, d = a4.shape
    blk = pl.BlockSpec((2, kb, n2, d), lambda i: (0, i, 0, 0))
    return pl.pallas_call(
        functools.partial(_mid_kernel, kb=kb),
        grid=(n1 // kb,),
        in_specs=[blk, blk, _const_spec(c["f2r"].shape), _const_spec(c["f2i"].shape),
                  _const_spec(c["twr"].shape), _const_spec(c["twi"].shape)],
        out_specs=blk,
        out_shape=jax.ShapeDtypeStruct(a4.shape, _BF16),
        compiler_params=_params("parallel"),
        name="fft_mid",
    )(a4, ak4, c["f2r"], c["f2i"], c["twr"], c["twi"])


def _outer_inv_kernel(minv_ref, mfwd_ref, b_ref, z_ref, gate_ref, bias_ref, zo_ref, ao_ref):
    y = _dot(minv_ref[...], b_ref[...])
    half = y.shape[0] // 2
    y3 = y.reshape(2, half, y.shape[1])
    zn = gate_ref[...] * (y3 + bias_ref[...] * z_ref[...])
    zo_ref[...] = zn
    if ao_ref is not None:
        ao_ref[...] = _dot(mfwd_ref[...], zn.reshape(y.shape).astype(_BF16)).astype(ao_ref.dtype)


def _outer_inv_last_kernel(minv_ref, b_ref, z_ref, gate_ref, bias_ref, zo_ref):
    _outer_inv_kernel(minv_ref, None, b_ref, z_ref, gate_ref, bias_ref, zo_ref, None)


def _outer_inv(minv, mfwd, b2, z3, gate3, bias_t, wc):
    p, r, c = z3.shape
    zspec = pl.BlockSpec((p, r, wc), lambda j: (0, 0, j))
    zshape = jax.ShapeDtypeStruct(z3.shape, _F32)
    common = [pl.BlockSpec((b2.shape[0], wc), lambda j: (0, j)), zspec, zspec,
              pl.BlockSpec((1, wc), lambda j: (0, 0), pipeline_mode=pl.Buffered(1))]
    if mfwd is None:
        return pl.pallas_call(
            _outer_inv_last_kernel,
            grid=(c // wc,),
            in_specs=[_const_spec(minv.shape)] + common,
            out_specs=zspec,
            out_shape=zshape,
            compiler_params=_params("parallel"),
            name="fft_outer_inv_last",
        )(minv, b2, z3, gate3, bias_t)
    rows = mfwd.shape[0]
    return pl.pallas_call(
        _outer_inv_kernel,
        grid=(c // wc,),
        in_specs=[_const_spec(minv.shape), _const_spec(mfwd.shape)] + common,
        out_specs=[zspec, pl.BlockSpec((rows, wc), lambda j: (0, j))],
        out_shape=[zshape, jax.ShapeDtypeStruct((rows, c), _BF16)],
        compiler_params=_params("parallel"),
        name="fft_outer_inv_fwd",
    )(minv, mfwd, b2, z3, gate3, bias_t)


def _pick(n, target):
    t = min(n, target)
    while n % t:
        t //= 2
    return t


def _hyena_mixer(x2d, batch, seq, w_in, w_short, fw1, fb1, fw2, fb2, fw3, fb3, freq, fw_out, d_bias):
    assert batch == 2, "batch 0 / batch 1 are packed as real / imaginary parts of one FFT"
    t, d = x2d.shape
    n = 2 * seq
    n1 = n // _N2
    half = n1 // 2
    hf = fw1.shape[1]
    tm = _pick(seq, 256)
    v, g1, g2 = _hyena_in(x2d, w_in.astype(_BF16), w_short, seq, tm)

    fl = jnp.concatenate([jnp.zeros((1,), _F32),
                          jnp.linspace(1e-4, _HY_BANDS - 1, _HY_BANDS, dtype=_F32),
                          jnp.linspace(1e-4, _HY_BANDS - 1, _HY_BANDS, dtype=_F32),
                          jnp.zeros((_LANES - _HY_EMB,), _F32)])[None, :]
    w1p = jnp.zeros((_LANES, hf), _F32).at[:_HY_EMB].set(fw1).astype(_BF16)
    deltas = jnp.abs(jnp.linspace(math.log(_HY_FAST) / _HY_TARGET, math.log(_HY_SLOW) / _HY_TARGET,
                                  d, dtype=_F32))[None, :]
    wo4 = fw_out.reshape(hf, 2, 2, d)
    wo2 = jnp.transpose(wo4, (2, 0, 1, 3)).reshape(2, hf, 2 * d).astype(_BF16)
    rows = _pick(seq, 512)
    kern = _hyena_kernel_rows(fl, w1p, fb1[None, :], fw2.astype(_BF16), fb2[None, :],
                              fw3.astype(_BF16), fb3[None, :], freq[None, :], wo2, deltas, seq, rows)

    c = _dft_consts(n1)
    cols = _N2 * d
    wc = _pick(cols, 2048)
    kb = _pick(n1, 4)
    fwd_c = c["fwd_c"].astype(_BF16)
    fwd_r = c["fwd_r"].astype(_BF16)
    inv_c = c["inv_c"].astype(_BF16)

    z3 = v.reshape(batch, half, cols)
    gates = (g1.reshape(batch, half, cols), g2.reshape(batch, half, cols))
    a = _outer_fwd(fwd_c, z3, wc)
    for o in range(2):
        ak = _outer_fwd(fwd_r, kern[o].reshape(1, n1, cols), wc)
        bsp = _mid(a.reshape(2, n1, _N2, d), ak.reshape(2, n1, _N2, d), c, kb)
        bias_t = jnp.tile(d_bias[o][None, :], (1, wc // d))
        if o == 0:
            z3, a = _outer_inv(inv_c, fwd_c, bsp.reshape(2 * n1, cols), z3, gates[o], bias_t, wc)
        else:
            z3 = _outer_inv(inv_c, None, bsp.reshape(2 * n1, cols), z3, gates[o], bias_t, wc)
    return z3.reshape(t, d)


def _mla_weights(w_in, w_uq, w_ukv):
    d = w_in.shape[0]
    r0 = _Q_LORA + _KV_LORA
    hr = _ROPE // 2
    zpad = jnp.zeros((d, 64), w_in.dtype)
    win = jnp.concatenate([w_in, zpad, w_in[:, r0 + hr:r0 + _ROPE], w_in[:, r0:r0 + hr], zpad], axis=1)
    wq3 = w_uq.reshape(_Q_LORA, _HEADS, _QK)
    wq_n = wq3[:, :, :_NOPE].reshape(_Q_LORA, _HEADS * _NOPE)
    wq_r = wq3[:, :, _NOPE:].reshape(_Q_LORA, _HEADS * _ROPE)
    wq_s = jnp.concatenate([wq3[:, :, _NOPE + hr:], wq3[:, :, _NOPE:_NOPE + hr]], axis=2)
    wqt = jnp.concatenate([wq_n, wq_r, wq_s.reshape(_Q_LORA, _HEADS * _ROPE)], axis=1).T
    wkv3 = w_ukv.reshape(_KV_LORA, _HEADS, _NOPE + _VDIM)
    wk = wkv3[:, :, :_NOPE].reshape(_KV_LORA, _HEADS * _NOPE)
    wvt = wkv3[:, :, _NOPE:].reshape(_KV_LORA, _HEADS * _VDIM).T
    return win.astype(_BF16), wqt.astype(_BF16), wk.astype(_BF16), wvt.astype(_BF16)


def _mla_mixer(x2d, positions, batch, seq, w_in, g_q, w_uq, g_kv, w_ukv):
    win, wqt, wk, wvt = _mla_weights(w_in, w_uq, w_ukv)
    inv = 1.0 / (_ROPE_THETA ** (jnp.arange(0, _ROPE, 2, dtype=_F32) / _ROPE))
    tm = _pick(seq, 512)
    invf = jnp.tile(inv, 2 * _LANES // _ROPE)[None, :]
    invb = jnp.broadcast_to(jnp.tile(inv, 2)[:, None], (_ROPE, tm))
    qt, k, vt = _mla_proj(x2d, positions.reshape(-1, 1), positions.reshape(1, -1), win,
                          g_q[None, :], g_kv[None, :], wqt, wk, wvt, invf, invb, batch, seq, tm)
    tk = _pick(seq, 2048)
    o = _attention(qt, k, vt, _pick(seq, 1024), tk, _pick(tk, 256))
    return o.reshape(batch * seq, _HEADS * _VDIM)


def kernel(x, positions, mla_w_in, mla_g_q, mla_w_uq, mla_g_kv, mla_w_ukv, mla_w_o, hy_w_in, hy_w_short, hy_fw1, hy_fb1, hy_fw2, hy_fb2, hy_fw3, hy_fb3, hy_freq, hy_fw_out, hy_d_bias, hy_w_o, ffn_w_up, ffn_w_conv, ffn_w_down, ln1_g, ln1_b, ln2_g, ln2_b):
    batch, seq, d = x.shape
    x2d = x.reshape(batch * seq, d)
    tm = _pick(seq, 256)
    for i in range(_DEPTH):
        j = i // 2
        if i % 2 == 0:
            a = _mla_mixer(x2d, positions, batch, seq, mla_w_in[j], mla_g_q[j], mla_w_uq[j],
                           mla_g_kv[j], mla_w_ukv[j])
            w_o = mla_w_o[j]
        else:
            a = _hyena_mixer(x2d, batch, seq, hy_w_in[j], hy_w_short[j], hy_fw1[j], hy_fb1[j], hy_fw2[j],
                             hy_fb2[j], hy_fw3[j], hy_fb3[j], hy_freq[j], hy_fw_out[j], hy_d_bias[j])
            w_o = hy_w_o[j]
        x2d = _proj_ln(a, x2d, w_o.astype(_BF16), ln1_g[i][None, :], ln1_b[i][None, :], tm)
        x2d = _conv_ffn(x2d, ffn_w_up[i].astype(_BF16), ffn_w_conv[i], ffn_w_down[i].astype(_BF16),
                        ln2_g[i][None, :], ln2_b[i][None, :], seq, tm)
    return x2d.reshape(batch, seq, d)
```

```python
import functools
import math

import numpy as np
import jax
import jax.numpy as jnp
from jax import lax
from jax.experimental import pallas as pl
from jax.experimental.pallas import tpu as pltpu

_F32 = jnp.float32
_BF16 = jnp.bfloat16

_HEADS = 8
_NOPE = 128
_ROPE = 64
_VDIM = 128
_QK = _NOPE + _ROPE
_Q_LORA = 384
_KV_LORA = 256
_ROPE_THETA = 10000.0
_HY_EMB = 33
_HY_BANDS = (_HY_EMB - 1) // 2
_HY_FAST, _HY_SLOW, _HY_TARGET = 0.3, 1.5, 1e-2
_NORM_EPS = 1e-5
_RMS_EPS = 1e-6
_DEPTH = 2
_ALPHA = (2.0 * _DEPTH) ** 0.25

_LANES = 128
_SUBLANES = 8
_VMEM_LIMIT = 56 * 1024 * 1024

_N2 = 128

_KSUB = 256


def _params(*sem, flags=None):
    return pltpu.CompilerParams(dimension_semantics=sem, vmem_limit_bytes=_VMEM_LIMIT, flags=flags)


def _const_spec(shape):
    nd = len(shape)
    return pl.BlockSpec(shape, lambda *_: (0,) * nd, pipeline_mode=pl.Buffered(1))


def _dot(a, b):
    return jnp.dot(a, b, preferred_element_type=_F32)


def _layer_norm(y, g, b):
    mu = jnp.mean(y, axis=-1, keepdims=True)
    yc = y - mu
    var = jnp.mean(yc * yc, axis=-1, keepdims=True)
    return yc * lax.rsqrt(var + _NORM_EPS) * g + b


def _mla_proj_kernel(x_ref, posc_ref, posr_ref, win_ref, gq_ref, gkv_ref, wqt_ref, wk_ref, wvt_ref,
                     invf_ref, invb_ref, qt_ref, k_ref, vt_ref, *, qscale):
    xb = x_ref[...].astype(_BF16)
    h = _dot(xb, win_ref[...])
    cq = h[:, :_Q_LORA]
    ckv = h[:, _Q_LORA:_Q_LORA + _KV_LORA]
    kr = h[:, 640:704]
    krs = h[:, 768:832]

    def rms(c, g):
        ms = jnp.mean(c * c, axis=-1, keepdims=True)
        return (c * lax.rsqrt(ms + _RMS_EPS) * g).astype(_BF16)

    cqn = rms(cq, gq_ref[...])
    ckvn = rms(ckv, gkv_ref[...])
    nt = (((1,), (1,)), ((), ()))
    qt = lax.dot_general(wqt_ref[...], cqn, nt, preferred_element_type=_F32)
    vt = lax.dot_general(wvt_ref[...], ckvn, nt, preferred_element_type=_F32)
    kn = _dot(ckvn, wk_ref[...])

    ang = posc_ref[...].astype(_F32) * invf_ref[...]
    lane = lax.broadcasted_iota(jnp.int32, (1, _LANES), 1)
    sgn = jnp.where(lane % _ROPE < _ROPE // 2, -1.0, 1.0)
    k_rope = (kr * jnp.cos(ang)[:, :_ROPE] + krs * (jnp.sin(ang) * sgn)[:, :_ROPE]).astype(_BF16)

    angt = invb_ref[...] * posr_ref[...].astype(_F32)
    row = lax.broadcasted_iota(jnp.int32, (_ROPE, 1), 0)
    ct = jnp.cos(angt)
    st = jnp.sin(angt) * jnp.where(row < _ROPE // 2, -1.0, 1.0)
    nh = _HEADS * _NOPE
    nr = _HEADS * _ROPE
    for hd in range(_HEADS):
        qt_ref[0, hd, :_NOPE, :] = (qt[hd * _NOPE:(hd + 1) * _NOPE] * qscale).astype(_BF16)
        r = qt[nh + hd * _ROPE:nh + (hd + 1) * _ROPE]
        rs = qt[nh + nr + hd * _ROPE:nh + nr + (hd + 1) * _ROPE]
        qt_ref[0, hd, _NOPE:, :] = ((r * ct + rs * st) * qscale).astype(_BF16)
        k_ref[0, hd, :, :_NOPE] = kn[:, hd * _NOPE:(hd + 1) * _NOPE].astype(_BF16)
        k_ref[0, hd, :, _NOPE:] = k_rope
        for c in range(vt_ref.shape[2]):
            vt_ref[0, hd, c] = vt[hd * _VDIM:(hd + 1) * _VDIM, c * _KSUB:(c + 1) * _KSUB].astype(_BF16)


def _mla_proj(x2d, posc, posr, win, gq, gkv, wqt, wk, wvt, invf, invb, batch, seq, tm):
    t, d = x2d.shape
    spb = seq // tm
    qscale = (_QK ** -0.5) * math.log2(math.e)
    consts = [win, gq, gkv, wqt, wk, wvt, invf, invb]
    return pl.pallas_call(
        functools.partial(_mla_proj_kernel, qscale=qscale),
        grid=(t // tm,),
        in_specs=[
            pl.BlockSpec((tm, d), lambda i: (i, 0)),
            pl.BlockSpec((tm, 1), lambda i: (i, 0)),
            pl.BlockSpec((1, tm), lambda i: (0, i)),
        ] + [_const_spec(c.shape) for c in consts],
        out_specs=[
            pl.BlockSpec((1, _HEADS, _QK, tm), lambda i: (i // spb, 0, 0, i % spb)),
            pl.BlockSpec((1, _HEADS, tm, _QK), lambda i: (i // spb, 0, i % spb, 0)),
            pl.BlockSpec((1, _HEADS, tm // _KSUB, _VDIM, _KSUB), lambda i: (i // spb, 0, i % spb, 0, 0)),
        ],
        out_shape=[
            jax.ShapeDtypeStruct((batch, _HEADS, _QK, seq), _BF16),
            jax.ShapeDtypeStruct((batch, _HEADS, seq, _QK), _BF16),
            jax.ShapeDtypeStruct((batch, _HEADS, seq // _KSUB, _VDIM, _KSUB), _BF16),
        ],
        compiler_params=_params("parallel"),
        name="mla_proj",
    )(x2d, posc, posr, *consts)


_ONES_ROWS = 16


def _attn_kernel(qt_ref, k_ref, vt_ref, o_ref, s_ref, p_ref, acc_ref, *, sub, nsub):
    ones = (lax.broadcasted_iota(jnp.int32, (_ONES_ROWS, sub), 0) == 0).astype(_BF16)

    def stage_a(j, slot):
        kj = k_ref[0, 0, pl.ds(pl.multiple_of(j * sub, sub), sub), :]
        st = _dot(kj, qt_ref[0, 0])
        s_ref[slot] = st
        return jnp.max(st, axis=0, keepdims=True)

    def stage_b(slot, m, cmax):
        m_new = jnp.maximum(m, cmax)
        p_ref[slot] = jnp.exp2(s_ref[slot] - m_new).astype(_BF16)
        return m_new, jnp.exp2(m - m_new)

    def stage_c(j, slot, alpha):
        vte = jnp.concatenate([vt_ref[0, 0, j], ones], axis=0)
        acc_ref[...] = alpha * acc_ref[...] + _dot(vte, p_ref[slot])

    acc_ref[...] = jnp.zeros(acc_ref.shape, _F32)
    m = jnp.full((1, qt_ref.shape[-1]), -jnp.inf, _F32)
    cm = stage_a(0, 0)
    m, alpha = stage_b(0, m, cm)
    cm = stage_a(1, 1)

    def step(slot, i, carry):
        m, cm_prev, al_prev = carry
        stage_c(i - 2, slot, al_prev)
        m, al = stage_b(1 - slot, m, cm_prev)
        cm_i = stage_a(i, slot)
        return m, cm_i, al

    def body(t, carry):
        return step(1, 2 * t + 3, step(0, 2 * t + 2, carry))

    m, cm, alpha = lax.fori_loop(0, (nsub - 2) // 2, body, (m, cm, alpha))
    stage_c(nsub - 2, 0, alpha)
    m, alpha = stage_b(1, m, cm)
    stage_c(nsub - 1, 1, alpha)

    acc = acc_ref[...]
    o = acc[:_VDIM] / acc[_VDIM:_VDIM + 1]
    o_ref[0] = o.T.astype(o_ref.dtype)


def _attention(qt, k, vt5, tq):
    b, h, s, _ = k.shape
    nsub, sub = vt5.shape[2], vt5.shape[4]
    assert nsub % 2 == 0 and nsub >= 2
    return pl.pallas_call(
        functools.partial(_attn_kernel, sub=sub, nsub=nsub),
        grid=(b, h, s // tq),
        in_specs=[
            pl.BlockSpec((1, 1, _QK, tq), lambda bi, hi, qi: (bi, hi, 0, qi)),
            pl.BlockSpec((1, 1, s, _QK), lambda bi, hi, qi: (bi, hi, 0, 0)),
            pl.BlockSpec((1, 1, nsub, _VDIM, sub), lambda bi, hi, qi: (bi, hi, 0, 0, 0)),
        ],
        out_specs=pl.BlockSpec((1, tq, _VDIM), lambda bi, hi, qi: (bi, qi, hi)),
        out_shape=jax.ShapeDtypeStruct((b, s, h * _VDIM), _BF16),
        scratch_shapes=[
            pltpu.VMEM((2, sub, tq), _F32),
            pltpu.VMEM((2, sub, tq), _BF16),
            pltpu.VMEM((_VDIM + _ONES_ROWS, tq), _F32),
        ],
        compiler_params=_params("parallel", "parallel", "parallel"),
        name="mla_attention",
    )(qt, k, vt5)


def _proj_ln_kernel(a_ref, x_ref, w_ref, g_ref, b_ref, o_ref):
    m = _dot(a_ref[...].astype(_BF16), w_ref[...])
    o_ref[...] = _layer_norm(_ALPHA * x_ref[...] + m, g_ref[...], b_ref[...])


def _proj_ln(a2d, x2d, w, g, b, tm):
    t, d = x2d.shape
    ka = a2d.shape[1]
    return pl.pallas_call(
        _proj_ln_kernel,
        grid=(t // tm,),
        in_specs=[
            pl.BlockSpec((tm, ka), lambda i: (i, 0)),
            pl.BlockSpec((tm, d), lambda i: (i, 0)),
            _const_spec(w.shape), _const_spec(g.shape), _const_spec(b.shape),
        ],
        out_specs=pl.BlockSpec((tm, d), lambda i: (i, 0)),
        out_shape=jax.ShapeDtypeStruct((t, d), _F32),
        compiler_params=_params("parallel"),
        name="proj_ln",
    )(a2d, x2d, w, g, b)


def _halo_specs(tm, d, nrows):
    r = tm // _SUBLANES
    last = nrows // _SUBLANES - 1
    return [
        pl.BlockSpec((_SUBLANES, d), lambda i: (jnp.maximum(i * r - 1, 0), 0)),
        pl.BlockSpec((tm, d), lambda i: (i, 0)),
        pl.BlockSpec((_SUBLANES, d), lambda i: (jnp.minimum((i + 1) * r, last), 0)),
    ]


def _assemble_halo(xp_ref, x_ref, xn_ref, xcat_ref, tm, tiles_per_seq):
    i = pl.program_id(0)
    first = (i % tiles_per_seq) == 0
    last = (i % tiles_per_seq) == tiles_per_seq - 1
    xcat_ref[0:_SUBLANES, :] = jnp.where(first, 0.0, xp_ref[...])
    xcat_ref[_SUBLANES:_SUBLANES + tm, :] = x_ref[...]
    xcat_ref[_SUBLANES + tm:, :] = jnp.where(last, 0.0, xn_ref[...])


def _dwconv3(h_ref, wc_ref, tm):
    lo = h_ref[pl.ds(_SUBLANES - 1, tm), :]
    mid = h_ref[pl.ds(_SUBLANES, tm), :]
    hi = h_ref[pl.ds(_SUBLANES + 1, tm), :]
    return lo * wc_ref[0:1, :] + mid * wc_ref[1:2, :] + hi * wc_ref[2:3, :]


def _conv_ffn_kernel(xp_ref, x_ref, xn_ref, wup_ref, wc_ref, wdn_ref, g_ref, b_ref, o_ref,
                     xcat_ref, h_ref, *, tm, tiles_per_seq, dff):
    _assemble_halo(xp_ref, x_ref, xn_ref, xcat_ref, tm, tiles_per_seq)
    h_ref[...] = _dot(xcat_ref[...].astype(_BF16), wup_ref[...])
    hc = _dwconv3(h_ref, wc_ref, tm)
    a = hc[:, :dff]
    gt = hc[:, dff:]
    act = (gt * jax.nn.sigmoid(gt) * a).astype(_BF16)
    f = _dot(act, wdn_ref[...])
    o_ref[...] = _layer_norm(_ALPHA * x_ref[...] + f, g_ref[...], b_ref[...])


def _conv_ffn(x2d, wup, wc, wdn, g, b, seq, tm):
    t, d = x2d.shape
    dff = wdn.shape[0]
    return pl.pallas_call(
        functools.partial(_conv_ffn_kernel, tm=tm, tiles_per_seq=seq // tm, dff=dff),
        grid=(t // tm,),
        in_specs=_halo_specs(tm, d, t) + [
            _const_spec(wup.shape), _const_spec(wc.shape), _const_spec(wdn.shape),
            _const_spec(g.shape), _const_spec(b.shape),
        ],
        out_specs=pl.BlockSpec((tm, d), lambda i: (i, 0)),
        out_shape=jax.ShapeDtypeStruct((t, d), _F32),
        scratch_shapes=[
            pltpu.VMEM((tm + 2 * _SUBLANES, d), _F32),
            pltpu.VMEM((tm + 2 * _SUBLANES, 2 * dff), _F32),
        ],
        compiler_params=_params("parallel"),
        name="conv_ffn",
    )(x2d, x2d, x2d, wup, wc, wdn, g, b)


def _hyena_in_kernel(xp_ref, x_ref, xn_ref, win_ref, wc_ref, v_ref, g1_ref, g2_ref,
                     xcat_ref, h_ref, *, tm, tiles_per_seq, d):
    _assemble_halo(xp_ref, x_ref, xn_ref, xcat_ref, tm, tiles_per_seq)
    h_ref[...] = _dot(xcat_ref[...].astype(_BF16), win_ref[...])
    u = _dwconv3(h_ref, wc_ref, tm)
    for a in range(tm // _N2):
        rows = slice(a * _N2, (a + 1) * _N2)
        for o_ref, c in ((v_ref, 0), (g1_ref, 1), (g2_ref, 2)):
            o_ref[:, 0, a] = u[rows, c * d:(c + 1) * d].reshape(_N2 // _SUBLANES, _SUBLANES, d)


def _hyena_in(x2d, win, wc, seq, tm):
    t, d = x2d.shape
    spb = seq // tm
    ng = _N2 // _SUBLANES
    out = jax.ShapeDtypeStruct((ng, t // seq, seq // _N2, _SUBLANES, d), _F32)
    ospec = pl.BlockSpec((ng, 1, tm // _N2, _SUBLANES, d), lambda i: (0, i // spb, i % spb, 0, 0))
    return pl.pallas_call(
        functools.partial(_hyena_in_kernel, tm=tm, tiles_per_seq=seq // tm, d=d),
        grid=(t // tm,),
        in_specs=_halo_specs(tm, d, t) + [_const_spec(win.shape), _const_spec(wc.shape)],
        out_specs=[ospec, ospec, ospec],
        out_shape=[out, out, out],
        scratch_shapes=[
            pltpu.VMEM((tm + 2 * _SUBLANES, d), _F32),
            pltpu.VMEM((tm + 2 * _SUBLANES, 3 * d), _F32),
        ],
        compiler_params=_params("parallel"),
        name="hyena_in",
    )(x2d, x2d, x2d, win, wc)


def _dft_consts(n1):
    n = n1 * _N2
    k = np.arange(n1)
    ang1 = -2.0 * np.pi * ((k[:, None] * k[None, :]) % n1) / n1
    f1r, f1i = np.cos(ang1), np.sin(ang1)
    half = n1 // 2
    fwd_c = np.block([[f1r[:, :half], -f1i[:, :half]], [f1i[:, :half], f1r[:, :half]]])
    fwd_r = np.concatenate([f1r, f1i], axis=0)
    inv_c = np.block([[f1r[:half], f1i[:half]], [-f1i[:half], f1r[:half]]]) / n
    k2 = np.arange(_N2)
    ang2 = -2.0 * np.pi * ((k2[:, None] * k2[None, :]) % _N2) / _N2
    angt = -2.0 * np.pi * ((k[:, None] * k2[None, :]) % n) / n
    f32 = lambda a: jnp.asarray(a.astype(np.float32))
    return dict(fwd_c=f32(fwd_c), fwd_r=f32(fwd_r), inv_c=f32(inv_c),
                f2r=f32(np.cos(ang2)), f2i=f32(np.sin(ang2)),
                twr=f32(np.cos(angt)), twi=f32(np.sin(angt)))


def _tile_row(ref, r):
    flat = ref.reshape(math.prod(ref.shape[:-1]), ref.shape[-1])
    return flat[pl.ds(r, flat.shape[0] // _SUBLANES, stride=_SUBLANES), :]


def _tiles(x):
    return x.reshape(x.shape[0] // _SUBLANES, _SUBLANES, x.shape[1])


def _filter_mlp_kernel(fl_ref, w1_ref, b1_ref, w2_ref, b2_ref, w3_ref, b3_ref, fr_ref, fr3_ref, h_ref,
                       *, rows, length):
    r = pl.program_id(0) * rows + lax.broadcasted_iota(jnp.int32, (rows, 1), 0)
    lag = jnp.where(r < length, r, 2 * length - r).astype(_F32)
    tpos = lag / (length - 1.0)
    a = (lag * (2.0 * math.pi / length)) * fl_ref[...]
    lane = lax.broadcasted_iota(jnp.int32, (rows, _LANES), 1)
    phase = jnp.where(lane <= _HY_BANDS, 0.5 * math.pi, math.pi)
    z = jnp.where(lane == 0, tpos, jnp.where(lane <= 2 * _HY_BANDS, jnp.sin(a + phase), 0.0))
    fr = fr_ref[...]
    h = jnp.sin(fr * (_dot(z.astype(_BF16), w1_ref[...]) + b1_ref[...]))
    h = jnp.sin(fr * (_dot(h.astype(_BF16), w2_ref[...]) + b2_ref[...]))
    h = jnp.sin(fr3_ref[...] * (_dot(h.astype(_BF16), w3_ref[...]) + b3_ref[...]))
    for a in range(rows // _N2):
        h_ref[:, a] = h[a * _N2:(a + 1) * _N2].reshape(_N2 // _SUBLANES, _SUBLANES, h.shape[1])


def _filter_mlp(fl, w1, b1, w2, b2, w3, b3, fr, fr3, length, rows):
    consts = [fl, w1, b1, w2, b2, w3, b3, fr, fr3]
    hf = w3.shape[1]
    return pl.pallas_call(
        functools.partial(_filter_mlp_kernel, rows=rows, length=length),
        grid=(2 * length // rows,),
        in_specs=[_const_spec(c.shape) for c in consts],
        out_specs=pl.BlockSpec((_N2 // _SUBLANES, rows // _N2, _SUBLANES, hf), lambda i: (0, i, 0, 0)),
        out_shape=jax.ShapeDtypeStruct((_N2 // _SUBLANES, 2 * length // _N2, _SUBLANES, hf), _F32),
        compiler_params=_params("parallel"),
        name="hyena_filter_mlp",
    )(*consts)


def _filter_fft_kernel(h_ref, wo_ref, dl_ref, mf_ref, ak_ref, *, jb, length):
    n1 = h_ref.shape[1]
    half = n1 // 2
    j0 = pl.program_id(0) * jb
    row = lax.broadcasted_iota(jnp.int32, (n1, 1), 0)
    dc = dl_ref.shape[1]

    def filter_rows(jj, o):
        r = _N2 * row + (j0 + jj)
        lag = jnp.where(r < length, r, 2 * length - r).astype(_F32)
        decay = jnp.exp(-(lag / (length - 1.0)) * dl_ref[...])
        hb = _tile_row(h_ref.at[jj // _SUBLANES], jj % _SUBLANES).astype(_BF16)
        out = jnp.concatenate([_dot(hb[:half], wo_ref[0, o]), _dot(hb[half:], wo_ref[1, o])], axis=0)
        lag0 = _dot(hb[0:_SUBLANES], wo_ref[1, o])[0:1]
        ko = jnp.where(r != length, out * decay, 0.0) + jnp.where(r == 0, lag0 * decay[0:1], 0.0)
        return ko.astype(_BF16)

    for jj in range(0, jb, 2):
        for o in range(2):
            res = _dot(mf_ref[...], jnp.concatenate([filter_rows(jj, o), filter_rows(jj + 1, o)], axis=1))
            ak_ref[o, :, jj] = _tiles(res[:, :dc])
            ak_ref[o, :, jj + 1] = _tiles(res[:, dc:])


def _filter_fft(h4, wo4, dl, mf, length, jb, dc):
    ng, n1, _, hf = h4.shape
    n2 = ng * _SUBLANES
    d = dl.shape[1]
    rows = mf.shape[0]
    return pl.pallas_call(
        functools.partial(_filter_fft_kernel, jb=jb, length=length),
        grid=(n2 // jb, d // dc),
        in_specs=[pl.BlockSpec((jb // _SUBLANES, n1, _SUBLANES, hf), lambda j, c: (j, 0, 0, 0)),
                  pl.BlockSpec((2, 2, hf, dc), lambda j, c: (0, 0, 0, c)),
                  pl.BlockSpec((1, dc), lambda j, c: (0, c)),
                  _const_spec(mf.shape)],
        out_specs=pl.BlockSpec((2, rows // _SUBLANES, jb, _SUBLANES, dc), lambda j, c: (0, 0, j, 0, c)),
        out_shape=jax.ShapeDtypeStruct((2, rows // _SUBLANES, n2, _SUBLANES, d), _F32),
        compiler_params=_params("parallel", "parallel"),
        name="hyena_filter_fft",
    )(h4, wo4, dl, mf)


def _outer_fwd_kernel(m_ref, x_ref, o_ref, *, jb):
    dc = x_ref.shape[-1]
    pick = lambda jj: _tile_row(x_ref.at[jj // _SUBLANES], jj % _SUBLANES)
    for jj in range(0, jb, 2):
        res = _dot(m_ref[...], jnp.concatenate([pick(jj), pick(jj + 1)], axis=1).astype(_BF16))
        o_ref[:, jj] = _tiles(res[:, :dc])
        o_ref[:, jj + 1] = _tiles(res[:, dc:])


def _outer_fwd(m, x5, jb, dc):
    ng, b, half, _, d = x5.shape
    n2 = ng * _SUBLANES
    rows = m.shape[0]
    return pl.pallas_call(
        functools.partial(_outer_fwd_kernel, jb=jb),
        grid=(n2 // jb, d // dc),
        in_specs=[_const_spec(m.shape),
                  pl.BlockSpec((jb // _SUBLANES, b, half, _SUBLANES, dc), lambda j, c: (j, 0, 0, 0, c))],
        out_specs=pl.BlockSpec((rows // _SUBLANES, jb, _SUBLANES, dc), lambda j, c: (0, j, 0, c)),
        out_shape=jax.ShapeDtypeStruct((rows // _SUBLANES, n2, _SUBLANES, d), _F32),
        compiler_params=_params("parallel", "parallel"),
        name="fft_outer_fwd",
    )(m, x5)


def _mid_kernel(*refs, kb, nl):
    ar, ai, kr, ki = (refs[t * nl:(t + 1) * nl] for t in range(4))
    f2r_ref, f2i_ref, twr_ref, twi_ref, b_ref, m2_ref, m2t_ref = refs[4 * nl:]
    i = pl.program_id(0)

    @pl.when(pl.program_id(1) == 0)
    def _():
        f2r = f2r_ref[...]
        f2i = f2i_ref[...]
        for j in range(kb):
            k1 = i * kb + j
            tr = twr_ref[pl.ds(k1, 1), :]
            ti = twi_ref[pl.ds(k1, 1), :]
            cr = f2r * tr - f2i * ti
            ci = f2r * ti + f2i * tr
            m2 = jnp.concatenate([jnp.concatenate([cr, -ci], axis=1),
                                  jnp.concatenate([ci, cr], axis=1)], axis=0)
            m2_ref[j] = m2.astype(_BF16)
            m2t_ref[j] = m2.T.astype(_BF16)

    for j in range(kb):
        pick = lambda rs: jnp.concatenate([_tile_row(r.at[j // _SUBLANES], j % _SUBLANES) for r in rs], axis=1)
        a = jnp.concatenate([pick(ar), pick(ai)], axis=0).astype(_BF16)
        ak = jnp.concatenate([pick(kr), pick(ki)], axis=0).astype(_BF16)
        x = _dot(m2_ref[j], a)
        kf = _dot(m2_ref[j], ak)
        xr, xi = x[:_N2], x[_N2:]
        kfr, kfi = kf[:_N2], kf[_N2:]
        y = jnp.concatenate([xr * kfr - xi * kfi, xr * kfi + xi * kfr], axis=0).astype(_BF16)
        bt = _dot(m2t_ref[j], y)
        b_ref[:, 0, j] = _tiles(bt[:_N2])
        b_ref[:, 1, j] = _tiles(bt[_N2:])


def _mid(a4, ak5, order, c, kb, nl):
    groups, n2, _, d = a4.shape
    n1 = groups * _SUBLANES // 2
    nk = n1 // kb
    kg = kb // _SUBLANES

    def chunks(block, index):
        return [pl.BlockSpec(block + (_LANES,), functools.partial(index, l)) for l in range(nl)]

    sig = (kg, n2, _SUBLANES)
    in_specs = (chunks(sig, lambda l, i, j: (i, 0, 0, nl * j + l))
                + chunks(sig, lambda l, i, j: (nk + i, 0, 0, nl * j + l))
                + chunks((None,) + sig, lambda l, i, j: (order, i, 0, 0, nl * j + l))
                + chunks((None,) + sig, lambda l, i, j: (order, nk + i, 0, 0, nl * j + l))
                + [_const_spec(c[name].shape) for name in ("f2r", "f2i", "twr", "twi")])
    return pl.pallas_call(
        functools.partial(_mid_kernel, kb=kb, nl=nl),
        grid=(nk, d // (nl * _LANES)),
        in_specs=in_specs,
        out_specs=pl.BlockSpec((n2 // _SUBLANES, 2, kb, _SUBLANES, nl * _LANES), lambda i, j: (0, 0, i, 0, j)),
        out_shape=jax.ShapeDtypeStruct((n2 // _SUBLANES, 2, n1, _SUBLANES, d), _F32),
        scratch_shapes=[pltpu.VMEM((kb, 2 * n2, 2 * n2), _BF16), pltpu.VMEM((kb, 2 * n2, 2 * n2), _BF16)],
        compiler_params=_params("parallel", "arbitrary"),
        name="fft_mid",
    )(*([a4] * (2 * nl) + [ak5] * (2 * nl)), c["f2r"], c["f2i"], c["twr"], c["twi"])


def _outer_inv_kernel(minv_ref, mfwd_ref, b_ref, z_ref, gate_ref, bias_ref, zo_ref, ao_ref, *, jb, z_slabs):
    dc = b_ref.shape[-1]
    pair = lambda f, jj: jnp.concatenate([f(jj), f(jj + 1)], axis=1)
    row_of = lambda ref: (lambda q: _tile_row(ref.at[q // _SUBLANES], q % _SUBLANES))
    bias = jnp.concatenate([bias_ref[...], bias_ref[...]], axis=1)
    for jj in range(0, jb, 2):
        bj = pair(row_of(b_ref), jj).astype(_BF16)
        y = _dot(minv_ref[...], bj)
        zj = pair((lambda q: z_ref[q]) if z_slabs else row_of(z_ref), jj)
        zn = pair(row_of(gate_ref), jj) * (y + bias * zj)
        zo_ref[jj] = zn[:, :dc]
        zo_ref[jj + 1] = zn[:, dc:]
        if ao_ref is not None:
            res = _dot(mfwd_ref[...], zn.astype(_BF16))
            ao_ref[:, jj] = _tiles(res[:, :dc])
            ao_ref[:, jj + 1] = _tiles(res[:, dc:])


def _outer_inv_last_kernel(minv_ref, b_ref, z_ref, gate_ref, bias_ref, zo_ref, *, jb, z_slabs):
    _outer_inv_kernel(minv_ref, None, b_ref, z_ref, gate_ref, bias_ref, zo_ref, None, jb=jb, z_slabs=z_slabs)


def _outer_inv(minv, mfwd, b5, z, gate5, bias, jb, dc):
    ng, b, half, _, d = gate5.shape
    n2 = ng * _SUBLANES
    jg = jb // _SUBLANES
    z_slabs = z.ndim == 3
    gspec = pl.BlockSpec((jg, b, half, _SUBLANES, dc), lambda j, c: (j, 0, 0, 0, c))
    zslab = pl.BlockSpec((jb, b * half, dc), lambda j, c: (j, 0, c))
    common = [pl.BlockSpec((jg,) + b5.shape[1:4] + (dc,), lambda j, c: (j, 0, 0, 0, c)),
              zslab if z_slabs else gspec, gspec, pl.BlockSpec((1, dc), lambda j, c: (0, c))]
    zshape = jax.ShapeDtypeStruct((n2, b * half, d), _F32)
    if mfwd is None:
        return pl.pallas_call(
            functools.partial(_outer_inv_last_kernel, jb=jb, z_slabs=z_slabs),
            grid=(n2 // jb, d // dc),
            in_specs=[_const_spec(minv.shape)] + common,
            out_specs=zslab,
            out_shape=zshape,
            compiler_params=_params("parallel", "parallel"),
            name="fft_outer_inv_last",
        )(minv, b5, z, gate5, bias)
    rows = mfwd.shape[0]
    return pl.pallas_call(
        functools.partial(_outer_inv_kernel, jb=jb, z_slabs=z_slabs),
        grid=(n2 // jb, d // dc),
        in_specs=[_const_spec(minv.shape), _const_spec(mfwd.shape)] + common,
        out_specs=[zslab, pl.BlockSpec((rows // _SUBLANES, jb, _SUBLANES, dc), lambda j, c: (0, j, 0, c))],
        out_shape=[zshape, jax.ShapeDtypeStruct((rows // _SUBLANES, n2, _SUBLANES, d), _F32)],
        compiler_params=_params("parallel", "parallel"),
        name="fft_outer_inv_fwd",
    )(minv, mfwd, b5, z, gate5, bias)


def _pick(n, target):
    t = min(n, target)
    while n % t:
        t //= 2
    return t


def _hyena_mixer(x2d, batch, seq, w_in, w_short, fw1, fb1, fw2, fb2, fw3, fb3, freq, fw_out, d_bias):
    assert batch == 2, "batch 0 / batch 1 are packed as real / imaginary parts of one FFT"
    t, d = x2d.shape
    n1 = 2 * seq // _N2
    half = n1 // 2
    hf = fw1.shape[1]
    tm = _pick(seq, 256)
    v, g1, g2 = _hyena_in(x2d, w_in.astype(_BF16), w_short, seq, tm)

    c = _dft_consts(n1)
    fwd_c = c["fwd_c"].astype(_BF16)
    fwd_r = c["fwd_r"].astype(_BF16)
    inv_c = c["inv_c"].astype(_BF16)
    jb = 2 * _SUBLANES
    dc = _LANES

    fl = jnp.concatenate([jnp.zeros((1,), _F32),
                          jnp.linspace(1e-4, _HY_BANDS - 1, _HY_BANDS, dtype=_F32),
                          jnp.linspace(1e-4, _HY_BANDS - 1, _HY_BANDS, dtype=_F32),
                          jnp.zeros((_LANES - _HY_EMB,), _F32)])[None, :]
    w1p = jnp.zeros((_LANES, hf), _F32).at[:_HY_EMB].set(fw1).astype(_BF16)
    deltas = jnp.abs(jnp.linspace(math.log(_HY_FAST) / _HY_TARGET, math.log(_HY_SLOW) / _HY_TARGET,
                                  d, dtype=_F32))[None, :]
    pad = lambda a, axis: jnp.pad(a, [(0, _LANES - a.shape[ax]) if ax == axis else (0, 0) for ax in range(a.ndim)])
    h3 = _filter_mlp(fl, w1p, fb1[None, :], fw2.astype(_BF16), fb2[None, :], pad(fw3, 1).astype(_BF16),
                     pad(fb3[None, :], 1), freq[None, :], pad(freq[None, :], 1), seq, _pick(2 * seq, 1024))
    wo4 = pad(jnp.transpose(fw_out.reshape(hf, 2, 2, d), (2, 1, 0, 3)), 2).astype(_BF16)
    ak = _filter_fft(h3, wo4, deltas, fwd_r, seq, jb, dc)

    kb = _SUBLANES
    nl = _pick(d // _LANES, 4)
    z = v
    gates = (g1, g2)
    a = _outer_fwd(fwd_c, z, jb, dc)
    for o in range(2):
        bsp = _mid(a, ak, o, c, kb, nl)
        bias = d_bias[o][None, :]
        if o == 0:
            z, a = _outer_inv(inv_c, fwd_c, bsp, z, gates[o], bias, jb, dc)
        else:
            z = _outer_inv(inv_c, None, bsp, z, gates[o], bias, jb, dc)
    return jnp.transpose(z.reshape(_N2, batch, half, d), (1, 2, 0, 3)).reshape(t, d)


def _mla_weights(w_in, w_uq, w_ukv):
    d = w_in.shape[0]
    r0 = _Q_LORA + _KV_LORA
    hr = _ROPE // 2
    zpad = jnp.zeros((d, 64), w_in.dtype)
    win = jnp.concatenate([w_in, zpad, w_in[:, r0 + hr:r0 + _ROPE], w_in[:, r0:r0 + hr], zpad], axis=1)
    wq3 = w_uq.reshape(_Q_LORA, _HEADS, _QK)
    wq_n = wq3[:, :, :_NOPE].reshape(_Q_LORA, _HEADS * _NOPE)
    wq_r = wq3[:, :, _NOPE:].reshape(_Q_LORA, _HEADS * _ROPE)
    wq_s = jnp.concatenate([wq3[:, :, _NOPE + hr:], wq3[:, :, _NOPE:_NOPE + hr]], axis=2)
    wqt = jnp.concatenate([wq_n, wq_r, wq_s.reshape(_Q_LORA, _HEADS * _ROPE)], axis=1).T
    wkv3 = w_ukv.reshape(_KV_LORA, _HEADS, _NOPE + _VDIM)
    wk = wkv3[:, :, :_NOPE].reshape(_KV_LORA, _HEADS * _NOPE)
    wvt = wkv3[:, :, _NOPE:].reshape(_KV_LORA, _HEADS * _VDIM).T
    return win.astype(_BF16), wqt.astype(_BF16), wk.astype(_BF16), wvt.astype(_BF16)


def _mla_mixer(x2d, positions, batch, seq, w_in, g_q, w_uq, g_kv, w_ukv):
    win, wqt, wk, wvt = _mla_weights(w_in, w_uq, w_ukv)
    inv = 1.0 / (_ROPE_THETA ** (jnp.arange(0, _ROPE, 2, dtype=_F32) / _ROPE))
    tm = _pick(seq, 512)
    invf = jnp.tile(inv, 2 * _LANES // _ROPE)[None, :]
    invb = jnp.broadcast_to(jnp.tile(inv, 2)[:, None], (_ROPE, tm))
    qt, k, vt = _mla_proj(x2d, positions.reshape(-1, 1), positions.reshape(1, -1), win,
                          g_q[None, :], g_kv[None, :], wqt, wk, wvt, invf, invb, batch, seq, tm)
    o = _attention(qt, k, vt, _pick(seq, 1024))
    return o.reshape(batch * seq, _HEADS * _VDIM)


def kernel(x, positions, mla_w_in, mla_g_q, mla_w_uq, mla_g_kv, mla_w_ukv, mla_w_o, hy_w_in, hy_w_short, hy_fw1, hy_fb1, hy_fw2, hy_fb2, hy_fw3, hy_fb3, hy_freq, hy_fw_out, hy_d_bias, hy_w_o, ffn_w_up, ffn_w_conv, ffn_w_down, ln1_g, ln1_b, ln2_g, ln2_b):
    batch, seq, d = x.shape
    x2d = x.reshape(batch * seq, d)
    tm = _pick(seq, 256)
    for i in range(_DEPTH):
        j = i // 2
        if i % 2 == 0:
            a = _mla_mixer(x2d, positions, batch, seq, mla_w_in[j], mla_g_q[j], mla_w_uq[j],
                           mla_g_kv[j], mla_w_ukv[j])
            w_o = mla_w_o[j]
        else:
            a = _hyena_mixer(x2d, batch, seq, hy_w_in[j], hy_w_short[j], hy_fw1[j], hy_fb1[j], hy_fw2[j],
                             hy_fb2[j], hy_fw3[j], hy_fb3[j], hy_freq[j], hy_fw_out[j], hy_d_bias[j])
            w_o = hy_w_o[j]
        x2d = _proj_ln(a, x2d, w_o.astype(_BF16), ln1_g[i][None, :], ln1_b[i][None, :], tm)
        x2d = _conv_ffn(x2d, ffn_w_up[i].astype(_BF16), ffn_w_conv[i], ffn_w_down[i].astype(_BF16),
                        ln2_g[i][None, :], ln2_b[i][None, :], seq, tm)
    return x2d.reshape(batch, seq, d)
```

```python
import functools
import math

import numpy as np
import jax
import jax.numpy as jnp
from jax import lax
from jax.experimental import pallas as pl
from jax.experimental.pallas import tpu as pltpu

_F32 = jnp.float32
_BF16 = jnp.bfloat16

_HEADS = 8
_NOPE = 128
_ROPE = 64
_VDIM = 128
_QK = _NOPE + _ROPE
_Q_LORA = 384
_KV_LORA = 256
_ROPE_THETA = 10000.0
_HY_EMB = 33
_HY_BANDS = (_HY_EMB - 1) // 2
_HY_FAST, _HY_SLOW, _HY_TARGET = 0.3, 1.5, 1e-2
_NORM_EPS = 1e-5
_RMS_EPS = 1e-6
_DEPTH = 2
_ALPHA = (2.0 * _DEPTH) ** 0.25

_LANES = 128
_SUBLANES = 8
_VMEM_LIMIT = 56 * 1024 * 1024

_N2 = 128

_KSUB = 256
_KPAD = 16


def _params(*sem, flags=None):
    return pltpu.CompilerParams(dimension_semantics=sem, vmem_limit_bytes=_VMEM_LIMIT, flags=flags)


def _const_spec(shape):
    nd = len(shape)
    return pl.BlockSpec(shape, lambda *_: (0,) * nd, pipeline_mode=pl.Buffered(1))


def _dot(a, b):
    return jnp.dot(a, b, preferred_element_type=_F32)


def _layer_norm(y, g, b):
    mu = jnp.mean(y, axis=-1, keepdims=True)
    yc = y - mu
    var = jnp.mean(yc * yc, axis=-1, keepdims=True)
    return yc * lax.rsqrt(var + _NORM_EPS) * g + b


def _mla_proj_kernel(x_ref, posc_ref, posr_ref, win_ref, gq_ref, gkv_ref, wqt_ref, wk_ref, wvt_ref,
                     invf_ref, invb_ref, qt_ref, k_ref, vt_ref, *, qscale):
    xb = x_ref[...].astype(_BF16)
    h = _dot(xb, win_ref[...])
    cq = h[:, :_Q_LORA]
    ckv = h[:, _Q_LORA:_Q_LORA + _KV_LORA]
    kr = h[:, 640:704]
    krs = h[:, 768:832]

    def rms(c, g):
        ms = jnp.mean(c * c, axis=-1, keepdims=True)
        return (c * lax.rsqrt(ms + _RMS_EPS) * g).astype(_BF16)

    cqn = rms(cq, gq_ref[...])
    ckvn = rms(ckv, gkv_ref[...])
    nt = (((1,), (1,)), ((), ()))
    qt = lax.dot_general(wqt_ref[...], cqn, nt, preferred_element_type=_F32)
    vt = lax.dot_general(wvt_ref[...], ckvn, nt, preferred_element_type=_F32)
    kn = _dot(ckvn, wk_ref[...])

    ang = posc_ref[...].astype(_F32) * invf_ref[...]
    lane = lax.broadcasted_iota(jnp.int32, (1, _LANES), 1)
    sgn = jnp.where(lane % _ROPE < _ROPE // 2, -1.0, 1.0)
    k_rope = (kr * jnp.cos(ang)[:, :_ROPE] + krs * (jnp.sin(ang) * sgn)[:, :_ROPE]).astype(_BF16)

    angt = invb_ref[...] * posr_ref[...].astype(_F32)
    row = lax.broadcasted_iota(jnp.int32, (_ROPE, 1), 0)
    ct = jnp.cos(angt)
    st = jnp.sin(angt) * jnp.where(row < _ROPE // 2, -1.0, 1.0)
    nh = _HEADS * _NOPE
    nr = _HEADS * _ROPE
    k_pad = (lax.broadcasted_iota(jnp.int32, (x_ref.shape[0], _KPAD), 1) == 0).astype(_BF16)
    for hd in range(_HEADS):
        qt_ref[0, hd, :_NOPE, :] = (qt[hd * _NOPE:(hd + 1) * _NOPE] * qscale).astype(_BF16)
        r = qt[nh + hd * _ROPE:nh + (hd + 1) * _ROPE]
        rs = qt[nh + nr + hd * _ROPE:nh + nr + (hd + 1) * _ROPE]
        qt_ref[0, hd, _NOPE:, :] = ((r * ct + rs * st) * qscale).astype(_BF16)
        k_ref[0, hd, :, :_NOPE] = kn[:, hd * _NOPE:(hd + 1) * _NOPE].astype(_BF16)
        k_ref[0, hd, :, _NOPE:_QK] = k_rope
        k_ref[0, hd, :, _QK:] = k_pad
        for c in range(vt_ref.shape[2]):
            vt_ref[0, hd, c] = vt[hd * _VDIM:(hd + 1) * _VDIM, c * _KSUB:(c + 1) * _KSUB].astype(_BF16)


def _mla_proj(x2d, posc, posr, win, gq, gkv, wqt, wk, wvt, invf, invb, batch, seq, tm):
    t, d = x2d.shape
    spb = seq // tm
    qscale = (_QK ** -0.5) * math.log2(math.e)
    consts = [win, gq, gkv, wqt, wk, wvt, invf, invb]
    return pl.pallas_call(
        functools.partial(_mla_proj_kernel, qscale=qscale),
        grid=(t // tm,),
        in_specs=[
            pl.BlockSpec((tm, d), lambda i: (i, 0)),
            pl.BlockSpec((tm, 1), lambda i: (i, 0)),
            pl.BlockSpec((1, tm), lambda i: (0, i)),
        ] + [_const_spec(c.shape) for c in consts],
        out_specs=[
            pl.BlockSpec((1, _HEADS, _QK, tm), lambda i: (i // spb, 0, 0, i % spb)),
            pl.BlockSpec((1, _HEADS, tm, _QK + _KPAD), lambda i: (i // spb, 0, i % spb, 0)),
            pl.BlockSpec((1, _HEADS, tm // _KSUB, _VDIM, _KSUB), lambda i: (i // spb, 0, i % spb, 0, 0)),
        ],
        out_shape=[
            jax.ShapeDtypeStruct((batch, _HEADS, _QK, seq), _BF16),
            jax.ShapeDtypeStruct((batch, _HEADS, seq, _QK + _KPAD), _BF16),
            jax.ShapeDtypeStruct((batch, _HEADS, seq // _KSUB, _VDIM, _KSUB), _BF16),
        ],
        compiler_params=_params("parallel"),
        name="mla_proj",
    )(x2d, posc, posr, *consts)


_ONES_ROWS = _KPAD
_KW = _QK + _KPAD
_TAU = 20.0


def _attn_kernel(qt_ref, k_ref, vt_ref, o_ref, qx_ref, r_ref, p_ref, acc_ref, *, sub, nsub, group):
    tq = qt_ref.shape[-1]
    ones = (lax.broadcasted_iota(jnp.int32, (_ONES_ROWS, sub), 0) == 0).astype(_BF16)
    row = lax.broadcasted_iota(jnp.int32, (_ONES_ROWS, tq), 0)

    def set_reference(r):
        r_ref[...] = r
        qx_ref[_QK:, :] = jnp.where(row == 0, -r, 0.0).astype(_BF16)

    def shifted_scores(j):
        kj = k_ref[0, 0, pl.ds(pl.multiple_of(j * sub, sub), sub), :]
        return _dot(kj, qx_ref[...])

    def probs(j, slot):
        st = shifted_scores(j)
        p_ref[slot] = jnp.exp2(st).astype(_BF16)
        return jnp.max(st, axis=0, keepdims=True)

    def accumulate(j, slot):
        vte = jnp.concatenate([vt_ref[0, 0, j], ones], axis=0)
        acc_ref[...] += _dot(vte, p_ref[slot])

    qx_ref[:_QK, :] = qt_ref[0, 0]
    set_reference(jnp.zeros((1, tq), _F32))
    acc_ref[...] = jnp.zeros(acc_ref.shape, _F32)
    first = jnp.max(shifted_scores(0), axis=0, keepdims=True)
    set_reference(first.astype(_BF16).astype(_F32))

    def body(t, carry):
        base = t * group
        cm = probs(base, 0)
        for g in range(1, group):
            cm = jnp.maximum(cm, probs(base + g, g))

        @pl.when(jnp.max(cm) > _TAU)
        def _():
            r_old = r_ref[...]
            r_new = (r_old + jnp.maximum(cm, 0.0)).astype(_BF16).astype(_F32)
            acc_ref[...] = acc_ref[...] * jnp.exp2(r_old - r_new)
            set_reference(r_new)
            for g in range(group):
                probs(base + g, g)

        for g in range(group):
            accumulate(base + g, g)
        return carry

    lax.fori_loop(0, nsub // group, body, 0)
    acc = acc_ref[...]
    o = acc[:_VDIM] / acc[_VDIM:_VDIM + 1]
    o_ref[0] = o.T.astype(o_ref.dtype)


def _attention(qt, k, vt5, tq, group):
    b, h, s, kw = k.shape
    nsub, sub = vt5.shape[2], vt5.shape[4]
    assert nsub % group == 0
    return pl.pallas_call(
        functools.partial(_attn_kernel, sub=sub, nsub=nsub, group=group),
        grid=(b, h, s // tq),
        in_specs=[
            pl.BlockSpec((1, 1, _QK, tq), lambda bi, hi, qi: (bi, hi, 0, qi)),
            pl.BlockSpec((1, 1, s, kw), lambda bi, hi, qi: (bi, hi, 0, 0)),
            pl.BlockSpec((1, 1, nsub, _VDIM, sub), lambda bi, hi, qi: (bi, hi, 0, 0, 0)),
        ],
        out_specs=pl.BlockSpec((1, tq, _VDIM), lambda bi, hi, qi: (bi, qi, hi)),
        out_shape=jax.ShapeDtypeStruct((b, s, h * _VDIM), _BF16),
        scratch_shapes=[
            pltpu.VMEM((kw, tq), _BF16),
            pltpu.VMEM((1, tq), _F32),
            pltpu.VMEM((group, sub, tq), _BF16),
            pltpu.VMEM((_VDIM + _ONES_ROWS, tq), _F32),
        ],
        compiler_params=_params("parallel", "parallel", "parallel"),
        name="mla_attention",
    )(qt, k, vt5)


def _proj_ln_kernel(a_ref, x_ref, w_ref, g_ref, b_ref, o_ref):
    m = _dot(a_ref[...].astype(_BF16), w_ref[...])
    o_ref[...] = _layer_norm(_ALPHA * x_ref[...] + m, g_ref[...], b_ref[...])


def _proj_ln(a2d, x2d, w, g, b, tm):
    t, d = x2d.shape
    ka = a2d.shape[1]
    return pl.pallas_call(
        _proj_ln_kernel,
        grid=(t // tm,),
        in_specs=[
            pl.BlockSpec((tm, ka), lambda i: (i, 0)),
            pl.BlockSpec((tm, d), lambda i: (i, 0)),
            _const_spec(w.shape), _const_spec(g.shape), _const_spec(b.shape),
        ],
        out_specs=pl.BlockSpec((tm, d), lambda i: (i, 0)),
        out_shape=jax.ShapeDtypeStruct((t, d), _F32),
        compiler_params=_params("parallel"),
        name="proj_ln",
    )(a2d, x2d, w, g, b)


def _halo_specs(tm, d, nrows):
    r = tm // _SUBLANES
    last = nrows // _SUBLANES - 1
    return [
        pl.BlockSpec((_SUBLANES, d), lambda i: (jnp.maximum(i * r - 1, 0), 0)),
        pl.BlockSpec((tm, d), lambda i: (i, 0)),
        pl.BlockSpec((_SUBLANES, d), lambda i: (jnp.minimum((i + 1) * r, last), 0)),
    ]


def _assemble_halo(xp_ref, x_ref, xn_ref, xcat_ref, tm, tiles_per_seq):
    i = pl.program_id(0)
    first = (i % tiles_per_seq) == 0
    last = (i % tiles_per_seq) == tiles_per_seq - 1
    xcat_ref[0:_SUBLANES, :] = jnp.where(first, 0.0, xp_ref[...])
    xcat_ref[_SUBLANES:_SUBLANES + tm, :] = x_ref[...]
    xcat_ref[_SUBLANES + tm:, :] = jnp.where(last, 0.0, xn_ref[...])


def _dwconv3(h_ref, wc_ref, tm):
    lo = h_ref[pl.ds(_SUBLANES - 1, tm), :]
    mid = h_ref[pl.ds(_SUBLANES, tm), :]
    hi = h_ref[pl.ds(_SUBLANES + 1, tm), :]
    return lo * wc_ref[0:1, :] + mid * wc_ref[1:2, :] + hi * wc_ref[2:3, :]


def _conv_ffn_kernel(xp_ref, x_ref, xn_ref, wup_ref, wc_ref, wdn_ref, g_ref, b_ref, o_ref,
                     xcat_ref, h_ref, *, tm, tiles_per_seq, dff):
    _assemble_halo(xp_ref, x_ref, xn_ref, xcat_ref, tm, tiles_per_seq)
    h_ref[...] = _dot(xcat_ref[...].astype(_BF16), wup_ref[...])
    hc = _dwconv3(h_ref, wc_ref, tm)
    a = hc[:, :dff]
    gt = hc[:, dff:]
    act = (gt * jax.nn.sigmoid(gt) * a).astype(_BF16)
    f = _dot(act, wdn_ref[...])
    o_ref[...] = _layer_norm(_ALPHA * x_ref[...] + f, g_ref[...], b_ref[...])


def _conv_ffn(x2d, wup, wc, wdn, g, b, seq, tm):
    t, d = x2d.shape
    dff = wdn.shape[0]
    return pl.pallas_call(
        functools.partial(_conv_ffn_kernel, tm=tm, tiles_per_seq=seq // tm, dff=dff),
        grid=(t // tm,),
        in_specs=_halo_specs(tm, d, t) + [
            _const_spec(wup.shape), _const_spec(wc.shape), _const_spec(wdn.shape),
            _const_spec(g.shape), _const_spec(b.shape),
        ],
        out_specs=pl.BlockSpec((tm, d), lambda i: (i, 0)),
        out_shape=jax.ShapeDtypeStruct((t, d), _F32),
        scratch_shapes=[
            pltpu.VMEM((tm + 2 * _SUBLANES, d), _F32),
            pltpu.VMEM((tm + 2 * _SUBLANES, 2 * dff), _F32),
        ],
        compiler_params=_params("parallel"),
        name="conv_ffn",
    )(x2d, x2d, x2d, wup, wc, wdn, g, b)


def _hyena_in_kernel(xp_ref, x_ref, xn_ref, win_ref, wc_ref, v_ref, g1_ref, g2_ref,
                     xcat_ref, h_ref, *, tm, tiles_per_seq, d):
    _assemble_halo(xp_ref, x_ref, xn_ref, xcat_ref, tm, tiles_per_seq)
    h_ref[...] = _dot(xcat_ref[...].astype(_BF16), win_ref[...])
    u = _dwconv3(h_ref, wc_ref, tm)
    for a in range(tm // _N2):
        rows = slice(a * _N2, (a + 1) * _N2)
        for o_ref, c in ((v_ref, 0), (g1_ref, 1), (g2_ref, 2)):
            o_ref[:, 0, a] = u[rows, c * d:(c + 1) * d].reshape(_N2 // _SUBLANES, _SUBLANES, d)


def _hyena_in(x2d, win, wc, seq, tm):
    t, d = x2d.shape
    spb = seq // tm
    ng = _N2 // _SUBLANES
    out = jax.ShapeDtypeStruct((ng, t // seq, seq // _N2, _SUBLANES, d), _F32)
    ospec = pl.BlockSpec((ng, 1, tm // _N2, _SUBLANES, d), lambda i: (0, i // spb, i % spb, 0, 0))
    return pl.pallas_call(
        functools.partial(_hyena_in_kernel, tm=tm, tiles_per_seq=seq // tm, d=d),
        grid=(t // tm,),
        in_specs=_halo_specs(tm, d, t) + [_const_spec(win.shape), _const_spec(wc.shape)],
        out_specs=[ospec, ospec, ospec],
        out_shape=[out, out, out],
        scratch_shapes=[
            pltpu.VMEM((tm + 2 * _SUBLANES, d), _F32),
            pltpu.VMEM((tm + 2 * _SUBLANES, 3 * d), _F32),
        ],
        compiler_params=_params("parallel"),
        name="hyena_in",
    )(x2d, x2d, x2d, win, wc)


def _dft_consts(n1):
    n = n1 * _N2
    k = np.arange(n1)
    ang1 = -2.0 * np.pi * ((k[:, None] * k[None, :]) % n1) / n1
    f1r, f1i = np.cos(ang1), np.sin(ang1)
    half = n1 // 2
    fwd_c = np.block([[f1r[:, :half], -f1i[:, :half]], [f1i[:, :half], f1r[:, :half]]])
    fwd_r = np.concatenate([f1r, f1i], axis=0)
    inv_c = np.block([[f1r[:half], f1i[:half]], [-f1i[:half], f1r[:half]]]) / n
    k2 = np.arange(_N2)
    ang2 = -2.0 * np.pi * ((k2[:, None] * k2[None, :]) % _N2) / _N2
    angt = -2.0 * np.pi * ((k[:, None] * k2[None, :]) % n) / n
    f32 = lambda a: jnp.asarray(a.astype(np.float32))
    return dict(fwd_c=f32(fwd_c), fwd_r=f32(fwd_r), inv_c=f32(inv_c),
                f2r=f32(np.cos(ang2)), f2i=f32(np.sin(ang2)),
                twr=f32(np.cos(angt)), twi=f32(np.sin(angt)))


def _tile_row(ref, r):
    flat = ref.reshape(math.prod(ref.shape[:-1]), ref.shape[-1])
    return flat[pl.ds(r, flat.shape[0] // _SUBLANES, stride=_SUBLANES), :]


def _tiles(x):
    return x.reshape(x.shape[0] // _SUBLANES, _SUBLANES, x.shape[1])


def _filter_mlp_kernel(fl_ref, w1_ref, b1_ref, w2_ref, b2_ref, w3_ref, b3_ref, fr_ref, fr3_ref, h_ref,
                       *, rows, length):
    r = pl.program_id(0) * rows + lax.broadcasted_iota(jnp.int32, (rows, 1), 0)
    lag = jnp.where(r < length, r, 2 * length - r).astype(_F32)
    tpos = lag / (length - 1.0)
    a = (lag * (2.0 * math.pi / length)) * fl_ref[...]
    lane = lax.broadcasted_iota(jnp.int32, (rows, _LANES), 1)
    phase = jnp.where(lane <= _HY_BANDS, 0.5 * math.pi, math.pi)
    z = jnp.where(lane == 0, tpos, jnp.where(lane <= 2 * _HY_BANDS, jnp.sin(a + phase), 0.0))
    fr = fr_ref[...]
    h = jnp.sin(fr * (_dot(z.astype(_BF16), w1_ref[...]) + b1_ref[...]))
    h = jnp.sin(fr * (_dot(h.astype(_BF16), w2_ref[...]) + b2_ref[...]))
    h = jnp.sin(fr3_ref[...] * (_dot(h.astype(_BF16), w3_ref[...]) + b3_ref[...]))
    for a in range(rows // _N2):
        h_ref[:, a] = h[a * _N2:(a + 1) * _N2].reshape(_N2 // _SUBLANES, _SUBLANES, h.shape[1])


def _filter_mlp(fl, w1, b1, w2, b2, w3, b3, fr, fr3, length, rows):
    consts = [fl, w1, b1, w2, b2, w3, b3, fr, fr3]
    hf = w3.shape[1]
    return pl.pallas_call(
        functools.partial(_filter_mlp_kernel, rows=rows, length=length),
        grid=(2 * length // rows,),
        in_specs=[_const_spec(c.shape) for c in consts],
        out_specs=pl.BlockSpec((_N2 // _SUBLANES, rows // _N2, _SUBLANES, hf), lambda i: (0, i, 0, 0)),
        out_shape=jax.ShapeDtypeStruct((_N2 // _SUBLANES, 2 * length // _N2, _SUBLANES, hf), _F32),
        compiler_params=_params("parallel"),
        name="hyena_filter_mlp",
    )(*consts)


def _filter_fft_kernel(h_ref, wo_ref, dl_ref, mf_ref, ak_ref, *, jb, length):
    n1 = h_ref.shape[1]
    half = n1 // 2
    j0 = pl.program_id(0) * jb
    row = lax.broadcasted_iota(jnp.int32, (n1, 1), 0)
    dc = dl_ref.shape[1]

    def filter_rows(jj, o):
        r = _N2 * row + (j0 + jj)
        lag = jnp.where(r < length, r, 2 * length - r).astype(_F32)
        decay = jnp.exp(-(lag / (length - 1.0)) * dl_ref[...])
        hb = _tile_row(h_ref.at[jj // _SUBLANES], jj % _SUBLANES).astype(_BF16)
        out = jnp.concatenate([_dot(hb[:half], wo_ref[0, o]), _dot(hb[half:], wo_ref[1, o])], axis=0)
        lag0 = _dot(hb[0:_SUBLANES], wo_ref[1, o])[0:1]
        ko = jnp.where(r != length, out * decay, 0.0) + jnp.where(r == 0, lag0 * decay[0:1], 0.0)
        return ko.astype(_BF16)

    for jj in range(0, jb, 2):
        for o in range(2):
            res = _dot(mf_ref[...], jnp.concatenate([filter_rows(jj, o), filter_rows(jj + 1, o)], axis=1))
            ak_ref[o, :, jj] = _tiles(res[:, :dc])
            ak_ref[o, :, jj + 1] = _tiles(res[:, dc:])


def _filter_fft(h4, wo4, dl, mf, length, jb, dc):
    ng, n1, _, hf = h4.shape
    n2 = ng * _SUBLANES
    d = dl.shape[1]
    rows = mf.shape[0]
    return pl.pallas_call(
        functools.partial(_filter_fft_kernel, jb=jb, length=length),
        grid=(n2 // jb, d // dc),
        in_specs=[pl.BlockSpec((jb // _SUBLANES, n1, _SUBLANES, hf), lambda j, c: (j, 0, 0, 0)),
                  pl.BlockSpec((2, 2, hf, dc), lambda j, c: (0, 0, 0, c)),
                  pl.BlockSpec((1, dc), lambda j, c: (0, c)),
                  _const_spec(mf.shape)],
        out_specs=pl.BlockSpec((2, rows // _SUBLANES, jb, _SUBLANES, dc), lambda j, c: (0, 0, j, 0, c)),
        out_shape=jax.ShapeDtypeStruct((2, rows // _SUBLANES, n2, _SUBLANES, d), _F32),
        compiler_params=_params("parallel", "parallel"),
        name="hyena_filter_fft",
    )(h4, wo4, dl, mf)


def _outer_fwd_kernel(m_ref, x_ref, o_ref, *, jb):
    dc = x_ref.shape[-1]
    pick = lambda jj: _tile_row(x_ref.at[jj // _SUBLANES], jj % _SUBLANES)
    for jj in range(0, jb, 2):
        res = _dot(m_ref[...], jnp.concatenate([pick(jj), pick(jj + 1)], axis=1).astype(_BF16))
        o_ref[:, jj] = _tiles(res[:, :dc])
        o_ref[:, jj + 1] = _tiles(res[:, dc:])


def _outer_fwd(m, x5, jb, dc):
    ng, b, half, _, d = x5.shape
    n2 = ng * _SUBLANES
    rows = m.shape[0]
    return pl.pallas_call(
        functools.partial(_outer_fwd_kernel, jb=jb),
        grid=(n2 // jb, d // dc),
        in_specs=[_const_spec(m.shape),
                  pl.BlockSpec((jb // _SUBLANES, b, half, _SUBLANES, dc), lambda j, c: (j, 0, 0, 0, c))],
        out_specs=pl.BlockSpec((rows // _SUBLANES, jb, _SUBLANES, dc), lambda j, c: (0, j, 0, c)),
        out_shape=jax.ShapeDtypeStruct((rows // _SUBLANES, n2, _SUBLANES, d), _F32),
        compiler_params=_params("parallel", "parallel"),
        name="fft_outer_fwd",
    )(m, x5)


def _mid_kernel(*refs, kb, nl):
    ar, ai, kr, ki = (refs[t * nl:(t + 1) * nl] for t in range(4))
    f2r_ref, f2i_ref, twr_ref, twi_ref, b_ref, m2_ref, m2t_ref = refs[4 * nl:]
    i = pl.program_id(0)

    @pl.when(pl.program_id(1) == 0)
    def _():
        f2r = f2r_ref[...]
        f2i = f2i_ref[...]
        for j in range(kb):
            k1 = i * kb + j
            tr = twr_ref[pl.ds(k1, 1), :]
            ti = twi_ref[pl.ds(k1, 1), :]
            cr = f2r * tr - f2i * ti
            ci = f2r * ti + f2i * tr
            m2 = jnp.concatenate([jnp.concatenate([cr, -ci], axis=1),
                                  jnp.concatenate([ci, cr], axis=1)], axis=0)
            m2_ref[j] = m2.astype(_BF16)
            m2t_ref[j] = m2.T.astype(_BF16)

    for j in range(kb):
        pick = lambda rs: jnp.concatenate([_tile_row(r.at[j // _SUBLANES], j % _SUBLANES) for r in rs], axis=1)
        a = jnp.concatenate([pick(ar), pick(ai)], axis=0).astype(_BF16)
        ak = jnp.concatenate([pick(kr), pick(ki)], axis=0).astype(_BF16)
        x = _dot(m2_ref[j], a)
        kf = _dot(m2_ref[j], ak)
        xr, xi = x[:_N2], x[_N2:]
        kfr, kfi = kf[:_N2], kf[_N2:]
        y = jnp.concatenate([xr * kfr - xi * kfi, xr * kfi + xi * kfr], axis=0).astype(_BF16)
        bt = _dot(m2t_ref[j], y)
        b_ref[:, 0, j] = _tiles(bt[:_N2])
        b_ref[:, 1, j] = _tiles(bt[_N2:])


def _mid(a4, ak5, order, c, kb, nl):
    groups, n2, _, d = a4.shape
    n1 = groups * _SUBLANES // 2
    nk = n1 // kb
    kg = kb // _SUBLANES

    def chunks(block, index):
        return [pl.BlockSpec(block + (_LANES,), functools.partial(index, l)) for l in range(nl)]

    sig = (kg, n2, _SUBLANES)
    in_specs = (chunks(sig, lambda l, i, j: (i, 0, 0, nl * j + l))
                + chunks(sig, lambda l, i, j: (nk + i, 0, 0, nl * j + l))
                + chunks((None,) + sig, lambda l, i, j: (order, i, 0, 0, nl * j + l))
                + chunks((None,) + sig, lambda l, i, j: (order, nk + i, 0, 0, nl * j + l))
                + [_const_spec(c[name].shape) for name in ("f2r", "f2i", "twr", "twi")])
    return pl.pallas_call(
        functools.partial(_mid_kernel, kb=kb, nl=nl),
        grid=(nk, d // (nl * _LANES)),
        in_specs=in_specs,
        out_specs=pl.BlockSpec((n2 // _SUBLANES, 2, kb, _SUBLANES, nl * _LANES), lambda i, j: (0, 0, i, 0, j)),
        out_shape=jax.ShapeDtypeStruct((n2 // _SUBLANES, 2, n1, _SUBLANES, d), _F32),
        scratch_shapes=[pltpu.VMEM((kb, 2 * n2, 2 * n2), _BF16), pltpu.VMEM((kb, 2 * n2, 2 * n2), _BF16)],
        compiler_params=_params("parallel", "arbitrary"),
        name="fft_mid",
    )(*([a4] * (2 * nl) + [ak5] * (2 * nl)), c["f2r"], c["f2i"], c["twr"], c["twi"])


def _outer_inv_kernel(minv_ref, mfwd_ref, b_ref, z_ref, gate_ref, bias_ref, zo_ref, ao_ref, *, jb, z_slabs):
    dc = b_ref.shape[-1]
    pair = lambda f, jj: jnp.concatenate([f(jj), f(jj + 1)], axis=1)
    row_of = lambda ref: (lambda q: _tile_row(ref.at[q // _SUBLANES], q % _SUBLANES))
    bias = jnp.concatenate([bias_ref[...], bias_ref[...]], axis=1)
    for jj in range(0, jb, 2):
        bj = pair(row_of(b_ref), jj).astype(_BF16)
        y = _dot(minv_ref[...], bj)
        zj = pair((lambda q: z_ref[q]) if z_slabs else row_of(z_ref), jj)
        zn = pair(row_of(gate_ref), jj) * (y + bias * zj)
        zo_ref[jj] = zn[:, :dc]
        zo_ref[jj + 1] = zn[:, dc:]
        if ao_ref is not None:
            res = _dot(mfwd_ref[...], zn.astype(_BF16))
            ao_ref[:, jj] = _tiles(res[:, :dc])
            ao_ref[:, jj + 1] = _tiles(res[:, dc:])


def _outer_inv_last_kernel(minv_ref, b_ref, z_ref, gate_ref, bias_ref, zo_ref, *, jb, z_slabs):
    _outer_inv_kernel(minv_ref, None, b_ref, z_ref, gate_ref, bias_ref, zo_ref, None, jb=jb, z_slabs=z_slabs)


def _outer_inv(minv, mfwd, b5, z, gate5, bias, jb, dc):
    ng, b, half, _, d = gate5.shape
    n2 = ng * _SUBLANES
    jg = jb // _SUBLANES
    z_slabs = z.ndim == 3
    gspec = pl.BlockSpec((jg, b, half, _SUBLANES, dc), lambda j, c: (j, 0, 0, 0, c))
    zslab = pl.BlockSpec((jb, b * half, dc), lambda j, c: (j, 0, c))
    common = [pl.BlockSpec((jg,) + b5.shape[1:4] + (dc,), lambda j, c: (j, 0, 0, 0, c)),
              zslab if z_slabs else gspec, gspec, pl.BlockSpec((1, dc), lambda j, c: (0, c))]
    zshape = jax.ShapeDtypeStruct((n2, b * half, d), _F32)
    if mfwd is None:
        return pl.pallas_call(
            functools.partial(_outer_inv_last_kernel, jb=jb, z_slabs=z_slabs),
            grid=(n2 // jb, d // dc),
            in_specs=[_const_spec(minv.shape)] + common,
            out_specs=zslab,
            out_shape=zshape,
            compiler_params=_params("parallel", "parallel"),
            name="fft_outer_inv_last",
        )(minv, b5, z, gate5, bias)
    rows = mfwd.shape[0]
    return pl.pallas_call(
        functools.partial(_outer_inv_kernel, jb=jb, z_slabs=z_slabs),
        grid=(n2 // jb, d // dc),
        in_specs=[_const_spec(minv.shape), _const_spec(mfwd.shape)] + common,
        out_specs=[zslab, pl.BlockSpec((rows // _SUBLANES, jb, _SUBLANES, dc), lambda j, c: (0, j, 0, c))],
        out_shape=[zshape, jax.ShapeDtypeStruct((rows // _SUBLANES, n2, _SUBLANES, d), _F32)],
        compiler_params=_params("parallel", "parallel"),
        name="fft_outer_inv_fwd",
    )(minv, mfwd, b5, z, gate5, bias)


def _pick(n, target):
    t = min(n, target)
    while n % t:
        t //= 2
    return t


def _hyena_mixer(x2d, batch, seq, w_in, w_short, fw1, fb1, fw2, fb2, fw3, fb3, freq, fw_out, d_bias):
    assert batch == 2, "batch 0 / batch 1 are packed as real / imaginary parts of one FFT"
    t, d = x2d.shape
    n1 = 2 * seq // _N2
    half = n1 // 2
    hf = fw1.shape[1]
    tm = _pick(seq, 256)
    v, g1, g2 = _hyena_in(x2d, w_in.astype(_BF16), w_short, seq, tm)

    c = _dft_consts(n1)
    fwd_c = c["fwd_c"].astype(_BF16)
    fwd_r = c["fwd_r"].astype(_BF16)
    inv_c = c["inv_c"].astype(_BF16)
    jb = 2 * _SUBLANES
    dc = _LANES

    fl = jnp.concatenate([jnp.zeros((1,), _F32),
                          jnp.linspace(1e-4, _HY_BANDS - 1, _HY_BANDS, dtype=_F32),
                          jnp.linspace(1e-4, _HY_BANDS - 1, _HY_BANDS, dtype=_F32),
                          jnp.zeros((_LANES - _HY_EMB,), _F32)])[None, :]
    w1p = jnp.zeros((_LANES, hf), _F32).at[:_HY_EMB].set(fw1).astype(_BF16)
    deltas = jnp.abs(jnp.linspace(math.log(_HY_FAST) / _HY_TARGET, math.log(_HY_SLOW) / _HY_TARGET,
                                  d, dtype=_F32))[None, :]
    pad = lambda a, axis: jnp.pad(a, [(0, _LANES - a.shape[ax]) if ax == axis else (0, 0) for ax in range(a.ndim)])
    h3 = _filter_mlp(fl, w1p, fb1[None, :], fw2.astype(_BF16), fb2[None, :], pad(fw3, 1).astype(_BF16),
                     pad(fb3[None, :], 1), freq[None, :], pad(freq[None, :], 1), seq, _pick(2 * seq, 1024))
    wo4 = pad(jnp.transpose(fw_out.reshape(hf, 2, 2, d), (2, 1, 0, 3)), 2).astype(_BF16)
    ak = _filter_fft(h3, wo4, deltas, fwd_r, seq, jb, dc)

    kb = _SUBLANES
    nl = _pick(d // _LANES, 4)
    z = v
    gates = (g1, g2)
    a = _outer_fwd(fwd_c, z, jb, dc)
    for o in range(2):
        bsp = _mid(a, ak, o, c, kb, nl)
        bias = d_bias[o][None, :]
        if o == 0:
            z, a = _outer_inv(inv_c, fwd_c, bsp, z, gates[o], bias, jb, dc)
        else:
            z = _outer_inv(inv_c, None, bsp, z, gates[o], bias, jb, dc)
    return jnp.transpose(z.reshape(_N2, batch, half, d), (1, 2, 0, 3)).reshape(t, d)


def _mla_weights(w_in, w_uq, w_ukv):
    d = w_in.shape[0]
    r0 = _Q_LORA + _KV_LORA
    hr = _ROPE // 2
    zpad = jnp.zeros((d, 64), w_in.dtype)
    win = jnp.concatenate([w_in, zpad, w_in[:, r0 + hr:r0 + _ROPE], w_in[:, r0:r0 + hr], zpad], axis=1)
    wq3 = w_uq.reshape(_Q_LORA, _HEADS, _QK)
    wq_n = wq3[:, :, :_NOPE].reshape(_Q_LORA, _HEADS * _NOPE)
    wq_r = wq3[:, :, _NOPE:].reshape(_Q_LORA, _HEADS * _ROPE)
    wq_s = jnp.concatenate([wq3[:, :, _NOPE + hr:], wq3[:, :, _NOPE:_NOPE + hr]], axis=2)
    wqt = jnp.concatenate([wq_n, wq_r, wq_s.reshape(_Q_LORA, _HEADS * _ROPE)], axis=1).T
    wkv3 = w_ukv.reshape(_KV_LORA, _HEADS, _NOPE + _VDIM)
    wk = wkv3[:, :, :_NOPE].reshape(_KV_LORA, _HEADS * _NOPE)
    wvt = wkv3[:, :, _NOPE:].reshape(_KV_LORA, _HEADS * _VDIM).T
    return win.astype(_BF16), wqt.astype(_BF16), wk.astype(_BF16), wvt.astype(_BF16)


def _mla_mixer(x2d, positions, batch, seq, w_in, g_q, w_uq, g_kv, w_ukv):
    win, wqt, wk, wvt = _mla_weights(w_in, w_uq, w_ukv)
    inv = 1.0 / (_ROPE_THETA ** (jnp.arange(0, _ROPE, 2, dtype=_F32) / _ROPE))
    tm = _pick(seq, 512)
    invf = jnp.tile(inv, 2 * _LANES // _ROPE)[None, :]
    invb = jnp.broadcast_to(jnp.tile(inv, 2)[:, None], (_ROPE, tm))
    qt, k, vt = _mla_proj(x2d, positions.reshape(-1, 1), positions.reshape(1, -1), win,
                          g_q[None, :], g_kv[None, :], wqt, wk, wvt, invf, invb, batch, seq, tm)
    o = _attention(qt, k, vt, _pick(seq, 1024), _pick(seq // _KSUB, 16))
    return o.reshape(batch * seq, _HEADS * _VDIM)


def kernel(x, positions, mla_w_in, mla_g_q, mla_w_uq, mla_g_kv, mla_w_ukv, mla_w_o, hy_w_in, hy_w_short, hy_fw1, hy_fb1, hy_fw2, hy_fb2, hy_fw3, hy_fb3, hy_freq, hy_fw_out, hy_d_bias, hy_w_o, ffn_w_up, ffn_w_conv, ffn_w_down, ln1_g, ln1_b, ln2_g, ln2_b):
    batch, seq, d = x.shape
    x2d = x.reshape(batch * seq, d)
    tm = _pick(seq, 256)
    for i in range(_DEPTH):
        j = i // 2
        if i % 2 == 0:
            a = _mla_mixer(x2d, positions, batch, seq, mla_w_in[j], mla_g_q[j], mla_w_uq[j],
                           mla_g_kv[j], mla_w_ukv[j])
            w_o = mla_w_o[j]
        else:
            a = _hyena_mixer(x2d, batch, seq, hy_w_in[j], hy_w_short[j], hy_fw1[j], hy_fb1[j], hy_fw2[j],
                             hy_fb2[j], hy_fw3[j], hy_fb3[j], hy_freq[j], hy_fw_out[j], hy_d_bias[j])
            w_o = hy_w_o[j]
        x2d = _proj_ln(a, x2d, w_o.astype(_BF16), ln1_g[i][None, :], ln1_b[i][None, :], tm)
        x2d = _conv_ffn(x2d, ffn_w_up[i].astype(_BF16), ffn_w_conv[i], ffn_w_down[i].astype(_BF16),
                        ln2_g[i][None, :], ln2_b[i][None, :], seq, tm)
    return x2d.reshape(batch, seq, d)
```

```python
import functools
import math

import numpy as np
import jax
import jax.numpy as jnp
from jax import lax
from jax.experimental import pallas as pl
from jax.experimental.pallas import tpu as pltpu

_F32 = jnp.float32
_BF16 = jnp.bfloat16

_HEADS = 8
_NOPE = 128
_ROPE = 64
_VDIM = 128
_QK = _NOPE + _ROPE
_Q_LORA = 384
_KV_LORA = 256
_ROPE_THETA = 10000.0
_HY_EMB = 33
_HY_BANDS = (_HY_EMB - 1) // 2
_HY_FAST, _HY_SLOW, _HY_TARGET = 0.3, 1.5, 1e-2
_NORM_EPS = 1e-5
_RMS_EPS = 1e-6
_DEPTH = 2
_ALPHA = (2.0 * _DEPTH) ** 0.25

_LANES = 128
_SUBLANES = 8
_VMEM_LIMIT = 56 * 1024 * 1024

_N2 = 128

_KSUB = 256
_KPAD = 16


def _params(*sem, flags=None):
    return pltpu.CompilerParams(dimension_semantics=sem, vmem_limit_bytes=_VMEM_LIMIT, flags=flags)


def _const_spec(shape):
    nd = len(shape)
    return pl.BlockSpec(shape, lambda *_: (0,) * nd, pipeline_mode=pl.Buffered(1))


def _dot(a, b):
    return jnp.dot(a, b, preferred_element_type=_F32)


def _layer_norm(y, g, b):
    mu = jnp.mean(y, axis=-1, keepdims=True)
    yc = y - mu
    var = jnp.mean(yc * yc, axis=-1, keepdims=True)
    return yc * lax.rsqrt(var + _NORM_EPS) * g + b


def _mla_proj_kernel(x_ref, posc_ref, posr_ref, win_ref, gq_ref, gkv_ref, wqt_ref, wk_ref, wvt_ref,
                     invf_ref, invb_ref, qt_ref, k_ref, vt_ref, *, qscale):
    xb = x_ref[...].astype(_BF16)
    h = _dot(xb, win_ref[...])
    cq = h[:, :_Q_LORA]
    ckv = h[:, _Q_LORA:_Q_LORA + _KV_LORA]
    kr = h[:, 640:704]
    krs = h[:, 768:832]

    def rms(c, g):
        ms = jnp.mean(c * c, axis=-1, keepdims=True)
        return (c * lax.rsqrt(ms + _RMS_EPS) * g).astype(_BF16)

    cqn = rms(cq, gq_ref[...])
    ckvn = rms(ckv, gkv_ref[...])
    nt = (((1,), (1,)), ((), ()))
    qt = lax.dot_general(wqt_ref[...], cqn, nt, preferred_element_type=_F32)
    vt = lax.dot_general(wvt_ref[...], ckvn, nt, preferred_element_type=_F32)
    kn = _dot(ckvn, wk_ref[...])

    ang = posc_ref[...].astype(_F32) * invf_ref[...]
    lane = lax.broadcasted_iota(jnp.int32, (1, _LANES), 1)
    sgn = jnp.where(lane % _ROPE < _ROPE // 2, -1.0, 1.0)
    k_rope = (kr * jnp.cos(ang)[:, :_ROPE] + krs * (jnp.sin(ang) * sgn)[:, :_ROPE]).astype(_BF16)

    angt = invb_ref[...] * posr_ref[...].astype(_F32)
    row = lax.broadcasted_iota(jnp.int32, (_ROPE, 1), 0)
    ct = jnp.cos(angt)
    st = jnp.sin(angt) * jnp.where(row < _ROPE // 2, -1.0, 1.0)
    nh = _HEADS * _NOPE
    nr = _HEADS * _ROPE
    k_pad = (lax.broadcasted_iota(jnp.int32, (x_ref.shape[0], _KPAD), 1) == 0).astype(_BF16)
    for hd in range(_HEADS):
        qt_ref[0, hd, :_NOPE, :] = (qt[hd * _NOPE:(hd + 1) * _NOPE] * qscale).astype(_BF16)
        r = qt[nh + hd * _ROPE:nh + (hd + 1) * _ROPE]
        rs = qt[nh + nr + hd * _ROPE:nh + nr + (hd + 1) * _ROPE]
        qt_ref[0, hd, _NOPE:, :] = ((r * ct + rs * st) * qscale).astype(_BF16)
        k_ref[0, hd, :, :_NOPE] = kn[:, hd * _NOPE:(hd + 1) * _NOPE].astype(_BF16)
        k_ref[0, hd, :, _NOPE:_QK] = k_rope
        k_ref[0, hd, :, _QK:] = k_pad
        for c in range(vt_ref.shape[2]):
            vt_ref[0, hd, c] = vt[hd * _VDIM:(hd + 1) * _VDIM, c * _KSUB:(c + 1) * _KSUB].astype(_BF16)


def _mla_proj(x2d, posc, posr, win, gq, gkv, wqt, wk, wvt, invf, invb, batch, seq, tm):
    t, d = x2d.shape
    spb = seq // tm
    qscale = (_QK ** -0.5) * math.log2(math.e)
    consts = [win, gq, gkv, wqt, wk, wvt, invf, invb]
    return pl.pallas_call(
        functools.partial(_mla_proj_kernel, qscale=qscale),
        grid=(t // tm,),
        in_specs=[
            pl.BlockSpec((tm, d), lambda i: (i, 0)),
            pl.BlockSpec((tm, 1), lambda i: (i, 0)),
            pl.BlockSpec((1, tm), lambda i: (0, i)),
        ] + [_const_spec(c.shape) for c in consts],
        out_specs=[
            pl.BlockSpec((1, _HEADS, _QK, tm), lambda i: (i // spb, 0, 0, i % spb)),
            pl.BlockSpec((1, _HEADS, tm, _QK + _KPAD), lambda i: (i // spb, 0, i % spb, 0)),
            pl.BlockSpec((1, _HEADS, tm // _KSUB, _VDIM, _KSUB), lambda i: (i // spb, 0, i % spb, 0, 0)),
        ],
        out_shape=[
            jax.ShapeDtypeStruct((batch, _HEADS, _QK, seq), _BF16),
            jax.ShapeDtypeStruct((batch, _HEADS, seq, _QK + _KPAD), _BF16),
            jax.ShapeDtypeStruct((batch, _HEADS, seq // _KSUB, _VDIM, _KSUB), _BF16),
        ],
        compiler_params=_params("parallel"),
        name="mla_proj",
    )(x2d, posc, posr, *consts)


_ONES_ROWS = _KPAD
_KW = _QK + _KPAD
_TAU = 20.0


def _attn_kernel(qt_ref, k_ref, vt_ref, o_ref, qx_ref, r_ref, p_ref, acc_ref, *, sub, nsub, group):
    tq = qt_ref.shape[-1]
    ones = (lax.broadcasted_iota(jnp.int32, (_ONES_ROWS, sub), 0) == 0).astype(_BF16)
    row = lax.broadcasted_iota(jnp.int32, (_ONES_ROWS, tq), 0)

    def set_reference(r):
        r_ref[...] = r
        qx_ref[_QK:, :] = jnp.where(row == 0, -r, 0.0).astype(_BF16)

    def shifted_scores(j):
        kj = k_ref[0, 0, pl.ds(pl.multiple_of(j * sub, sub), sub), :]
        return _dot(kj, qx_ref[...])

    def probs(j, slot):
        st = shifted_scores(j)
        p_ref[slot] = jnp.exp2(st).astype(_BF16)
        return jnp.max(st, axis=0, keepdims=True)

    def accumulate(j, slot):
        vte = jnp.concatenate([vt_ref[0, 0, j], ones], axis=0)
        acc_ref[...] += _dot(vte, p_ref[slot])

    qx_ref[:_QK, :] = qt_ref[0, 0]
    set_reference(jnp.zeros((1, tq), _F32))
    acc_ref[...] = jnp.zeros(acc_ref.shape, _F32)
    first = jnp.max(shifted_scores(0), axis=0, keepdims=True)
    set_reference(first.astype(_BF16).astype(_F32))

    def body(t, carry):
        base = t * group
        cm = probs(base, 0)
        for g in range(1, group):
            cm = jnp.maximum(cm, probs(base + g, g))

        @pl.when(jnp.max(cm) > _TAU)
        def _():
            r_old = r_ref[...]
            r_new = (r_old + jnp.maximum(cm, 0.0)).astype(_BF16).astype(_F32)
            acc_ref[...] = acc_ref[...] * jnp.exp2(r_old - r_new)
            set_reference(r_new)
            for g in range(group):
                probs(base + g, g)

        for g in range(group):
            accumulate(base + g, g)
        return carry

    lax.fori_loop(0, nsub // group, body, 0)
    acc = acc_ref[...]
    o = acc[:_VDIM] / acc[_VDIM:_VDIM + 1]
    o_ref[0] = o.T.astype(o_ref.dtype)


def _attention(qt, k, vt5, tq, group):
    b, h, s, kw = k.shape
    nsub, sub = vt5.shape[2], vt5.shape[4]
    assert nsub % group == 0
    return pl.pallas_call(
        functools.partial(_attn_kernel, sub=sub, nsub=nsub, group=group),
        grid=(b, h, s // tq),
        in_specs=[
            pl.BlockSpec((1, 1, _QK, tq), lambda bi, hi, qi: (bi, hi, 0, qi)),
            pl.BlockSpec((1, 1, s, kw), lambda bi, hi, qi: (bi, hi, 0, 0)),
            pl.BlockSpec((1, 1, nsub, _VDIM, sub), lambda bi, hi, qi: (bi, hi, 0, 0, 0)),
        ],
        out_specs=pl.BlockSpec((1, tq, _VDIM), lambda bi, hi, qi: (bi, qi, hi)),
        out_shape=jax.ShapeDtypeStruct((b, s, h * _VDIM), _BF16),
        scratch_shapes=[
            pltpu.VMEM((kw, tq), _BF16),
            pltpu.VMEM((1, tq), _F32),
            pltpu.VMEM((group, sub, tq), _BF16),
            pltpu.VMEM((_VDIM + _ONES_ROWS, tq), _F32),
        ],
        compiler_params=_params("parallel", "parallel", "parallel"),
        name="mla_attention",
    )(qt, k, vt5)


def _proj_ln_kernel(a_ref, x_ref, w_ref, g_ref, b_ref, o_ref):
    m = _dot(a_ref[...].astype(_BF16), w_ref[...])
    o_ref[...] = _layer_norm(_ALPHA * x_ref[...] + m, g_ref[...], b_ref[...])


def _proj_ln(a2d, x2d, w, g, b, tm):
    t, d = x2d.shape
    ka = a2d.shape[1]
    return pl.pallas_call(
        _proj_ln_kernel,
        grid=(t // tm,),
        in_specs=[
            pl.BlockSpec((tm, ka), lambda i: (i, 0)),
            pl.BlockSpec((tm, d), lambda i: (i, 0)),
            _const_spec(w.shape), _const_spec(g.shape), _const_spec(b.shape),
        ],
        out_specs=pl.BlockSpec((tm, d), lambda i: (i, 0)),
        out_shape=jax.ShapeDtypeStruct((t, d), _F32),
        compiler_params=_params("parallel"),
        name="proj_ln",
    )(a2d, x2d, w, g, b)


def _halo_specs(tm, d, nrows):
    r = tm // _SUBLANES
    last = nrows // _SUBLANES - 1
    return [
        pl.BlockSpec((_SUBLANES, d), lambda i: (jnp.maximum(i * r - 1, 0), 0)),
        pl.BlockSpec((tm, d), lambda i: (i, 0)),
        pl.BlockSpec((_SUBLANES, d), lambda i: (jnp.minimum((i + 1) * r, last), 0)),
    ]


def _assemble_halo(xp_ref, x_ref, xn_ref, xcat_ref, tm, tiles_per_seq):
    i = pl.program_id(0)
    first = (i % tiles_per_seq) == 0
    last = (i % tiles_per_seq) == tiles_per_seq - 1
    xcat_ref[0:_SUBLANES, :] = jnp.where(first, 0.0, xp_ref[...])
    xcat_ref[_SUBLANES:_SUBLANES + tm, :] = x_ref[...]
    xcat_ref[_SUBLANES + tm:, :] = jnp.where(last, 0.0, xn_ref[...])


def _dwconv3(h_ref, wc_ref, tm):
    lo = h_ref[pl.ds(_SUBLANES - 1, tm), :]
    mid = h_ref[pl.ds(_SUBLANES, tm), :]
    hi = h_ref[pl.ds(_SUBLANES + 1, tm), :]
    return lo * wc_ref[0:1, :] + mid * wc_ref[1:2, :] + hi * wc_ref[2:3, :]


def _conv_ffn_kernel(xp_ref, x_ref, xn_ref, wup_ref, wc_ref, wdn_ref, g_ref, b_ref, o_ref,
                     xcat_ref, h_ref, *, tm, tiles_per_seq, dff):
    _assemble_halo(xp_ref, x_ref, xn_ref, xcat_ref, tm, tiles_per_seq)
    h_ref[...] = _dot(xcat_ref[...].astype(_BF16), wup_ref[...])
    hc = _dwconv3(h_ref, wc_ref, tm)
    a = hc[:, :dff]
    gt = hc[:, dff:]
    act = (gt * jax.nn.sigmoid(gt) * a).astype(_BF16)
    f = _dot(act, wdn_ref[...])
    o_ref[...] = _layer_norm(_ALPHA * x_ref[...] + f, g_ref[...], b_ref[...])


def _conv_ffn(x2d, wup, wc, wdn, g, b, seq, tm):
    t, d = x2d.shape
    dff = wdn.shape[0]
    return pl.pallas_call(
        functools.partial(_conv_ffn_kernel, tm=tm, tiles_per_seq=seq // tm, dff=dff),
        grid=(t // tm,),
        in_specs=_halo_specs(tm, d, t) + [
            _const_spec(wup.shape), _const_spec(wc.shape), _const_spec(wdn.shape),
            _const_spec(g.shape), _const_spec(b.shape),
        ],
        out_specs=pl.BlockSpec((tm, d), lambda i: (i, 0)),
        out_shape=jax.ShapeDtypeStruct((t, d), _F32),
        scratch_shapes=[
            pltpu.VMEM((tm + 2 * _SUBLANES, d), _F32),
            pltpu.VMEM((tm + 2 * _SUBLANES, 2 * dff), _F32),
        ],
        compiler_params=_params("parallel"),
        name="conv_ffn",
    )(x2d, x2d, x2d, wup, wc, wdn, g, b)


def _hyena_in_kernel(xp_ref, x_ref, xn_ref, win_ref, wc_ref, v_ref, g1_ref, g2_ref,
                     xcat_ref, h_ref, *, tm, tiles_per_seq, d):
    _assemble_halo(xp_ref, x_ref, xn_ref, xcat_ref, tm, tiles_per_seq)
    h_ref[...] = _dot(xcat_ref[...].astype(_BF16), win_ref[...])
    u = _dwconv3(h_ref, wc_ref, tm)
    for a in range(tm // _N2):
        rows = slice(a * _N2, (a + 1) * _N2)
        for o_ref, c in ((v_ref, 0), (g1_ref, 1), (g2_ref, 2)):
            o_ref[:, 0, a] = u[rows, c * d:(c + 1) * d].reshape(_N2 // _SUBLANES, _SUBLANES, d)


def _hyena_in(x2d, win, wc, seq, tm):
    t, d = x2d.shape
    spb = seq // tm
    ng = _N2 // _SUBLANES
    out = jax.ShapeDtypeStruct((ng, t // seq, seq // _N2, _SUBLANES, d), _F32)
    ospec = pl.BlockSpec((ng, 1, tm // _N2, _SUBLANES, d), lambda i: (0, i // spb, i % spb, 0, 0))
    return pl.pallas_call(
        functools.partial(_hyena_in_kernel, tm=tm, tiles_per_seq=seq // tm, d=d),
        grid=(t // tm,),
        in_specs=_halo_specs(tm, d, t) + [_const_spec(win.shape), _const_spec(wc.shape)],
        out_specs=[ospec, ospec, ospec],
        out_shape=[out, out, out],
        scratch_shapes=[
            pltpu.VMEM((tm + 2 * _SUBLANES, d), _F32),
            pltpu.VMEM((tm + 2 * _SUBLANES, 3 * d), _F32),
        ],
        compiler_params=_params("parallel"),
        name="hyena_in",
    )(x2d, x2d, x2d, win, wc)


def _dft_consts(n1):
    n = n1 * _N2
    k = np.arange(n1)
    ang1 = -2.0 * np.pi * ((k[:, None] * k[None, :]) % n1) / n1
    f1r, f1i = np.cos(ang1), np.sin(ang1)
    half = n1 // 2
    fwd_c = np.block([[f1r[:, :half], -f1i[:, :half]], [f1i[:, :half], f1r[:, :half]]])
    fwd_r = np.concatenate([f1r, f1i], axis=0)
    inv_c = np.block([[f1r[:half], f1i[:half]], [-f1i[:half], f1r[:half]]]) / n
    k2 = np.arange(_N2)
    ang2 = -2.0 * np.pi * ((k2[:, None] * k2[None, :]) % _N2) / _N2
    angt = -2.0 * np.pi * ((k[:, None] * k2[None, :]) % n) / n
    f32 = lambda a: jnp.asarray(a.astype(np.float32))
    return dict(fwd_c=f32(fwd_c), fwd_r=f32(fwd_r), inv_c=f32(inv_c),
                f2r=f32(np.cos(ang2)), f2i=f32(np.sin(ang2)),
                twr=f32(np.cos(angt)), twi=f32(np.sin(angt)))


def _tile_row(ref, r):
    flat = ref.reshape(math.prod(ref.shape[:-1]), ref.shape[-1])
    return flat[pl.ds(r, flat.shape[0] // _SUBLANES, stride=_SUBLANES), :]


def _tiles(x):
    return x.reshape(x.shape[0] // _SUBLANES, _SUBLANES, x.shape[1])


def _filter_mlp_kernel(fl_ref, w1_ref, b1_ref, w2_ref, b2_ref, w3_ref, b3_ref, fr_ref, fr3_ref, h_ref,
                       *, rows, length):
    r = pl.program_id(0) * rows + lax.broadcasted_iota(jnp.int32, (rows, 1), 0)
    lag = jnp.where(r < length, r, 2 * length - r).astype(_F32)
    tpos = lag / (length - 1.0)
    a = (lag * (2.0 * math.pi / length)) * fl_ref[...]
    lane = lax.broadcasted_iota(jnp.int32, (rows, _LANES), 1)
    phase = jnp.where(lane <= _HY_BANDS, 0.5 * math.pi, math.pi)
    z = jnp.where(lane == 0, tpos, jnp.where(lane <= 2 * _HY_BANDS, jnp.sin(a + phase), 0.0))
    fr = fr_ref[...]
    h = jnp.sin(fr * (_dot(z.astype(_BF16), w1_ref[...]) + b1_ref[...]))
    h = jnp.sin(fr * (_dot(h.astype(_BF16), w2_ref[...]) + b2_ref[...]))
    h = jnp.sin(fr3_ref[...] * (_dot(h.astype(_BF16), w3_ref[...]) + b3_ref[...]))
    for a in range(rows // _N2):
        h_ref[:, a] = h[a * _N2:(a + 1) * _N2].reshape(_N2 // _SUBLANES, _SUBLANES, h.shape[1])


def _filter_mlp(fl, w1, b1, w2, b2, w3, b3, fr, fr3, length, rows):
    consts = [fl, w1, b1, w2, b2, w3, b3, fr, fr3]
    hf = w3.shape[1]
    return pl.pallas_call(
        functools.partial(_filter_mlp_kernel, rows=rows, length=length),
        grid=(2 * length // rows,),
        in_specs=[_const_spec(c.shape) for c in consts],
        out_specs=pl.BlockSpec((_N2 // _SUBLANES, rows // _N2, _SUBLANES, hf), lambda i: (0, i, 0, 0)),
        out_shape=jax.ShapeDtypeStruct((_N2 // _SUBLANES, 2 * length // _N2, _SUBLANES, hf), _F32),
        compiler_params=_params("parallel"),
        name="hyena_filter_mlp",
    )(*consts)


def _filter_fft_kernel(h_ref, wo_ref, dl_ref, mf_ref, ak_ref, *, jb, length):
    n1 = h_ref.shape[1]
    half = n1 // 2
    j0 = pl.program_id(0) * jb
    row = lax.broadcasted_iota(jnp.int32, (n1, 1), 0)
    dc = dl_ref.shape[1]

    def filter_rows(jj, o):
        r = _N2 * row + (j0 + jj)
        lag = jnp.where(r < length, r, 2 * length - r).astype(_F32)
        decay = jnp.exp(-(lag / (length - 1.0)) * dl_ref[...])
        hb = _tile_row(h_ref.at[jj // _SUBLANES], jj % _SUBLANES).astype(_BF16)
        out = jnp.concatenate([_dot(hb[:half], wo_ref[0, o]), _dot(hb[half:], wo_ref[1, o])], axis=0)
        lag0 = _dot(hb[0:_SUBLANES], wo_ref[1, o])[0:1]
        ko = jnp.where(r != length, out * decay, 0.0) + jnp.where(r == 0, lag0 * decay[0:1], 0.0)
        return ko.astype(_BF16)

    for jj in range(0, jb, 2):
        for o in range(2):
            res = _dot(mf_ref[...], jnp.concatenate([filter_rows(jj, o), filter_rows(jj + 1, o)], axis=1))
            ak_ref[o, :, jj] = _tiles(res[:, :dc])
            ak_ref[o, :, jj + 1] = _tiles(res[:, dc:])


def _filter_fft(h4, wo4, dl, mf, length, jb, dc):
    ng, n1, _, hf = h4.shape
    n2 = ng * _SUBLANES
    d = dl.shape[1]
    rows = mf.shape[0]
    return pl.pallas_call(
        functools.partial(_filter_fft_kernel, jb=jb, length=length),
        grid=(n2 // jb, d // dc),
        in_specs=[pl.BlockSpec((jb // _SUBLANES, n1, _SUBLANES, hf), lambda j, c: (j, 0, 0, 0)),
                  pl.BlockSpec((2, 2, hf, dc), lambda j, c: (0, 0, 0, c)),
                  pl.BlockSpec((1, dc), lambda j, c: (0, c)),
                  _const_spec(mf.shape)],
        out_specs=pl.BlockSpec((2, rows // _SUBLANES, jb, _SUBLANES, dc), lambda j, c: (0, 0, j, 0, c)),
        out_shape=jax.ShapeDtypeStruct((2, rows // _SUBLANES, n2, _SUBLANES, d), _F32),
        compiler_params=_params("parallel", "parallel"),
        name="hyena_filter_fft",
    )(h4, wo4, dl, mf)


def _outer_fwd_kernel(m_ref, x_ref, o_ref, *, jb):
    dc = x_ref.shape[-1]
    pick = lambda jj: _tile_row(x_ref.at[jj // _SUBLANES], jj % _SUBLANES)
    for jj in range(0, jb, 2):
        res = _dot(m_ref[...], jnp.concatenate([pick(jj), pick(jj + 1)], axis=1).astype(_BF16))
        o_ref[:, jj] = _tiles(res[:, :dc])
        o_ref[:, jj + 1] = _tiles(res[:, dc:])


def _outer_fwd(m, x5, jb, dc):
    ng, b, half, _, d = x5.shape
    n2 = ng * _SUBLANES
    rows = m.shape[0]
    return pl.pallas_call(
        functools.partial(_outer_fwd_kernel, jb=jb),
        grid=(n2 // jb, d // dc),
        in_specs=[_const_spec(m.shape),
                  pl.BlockSpec((jb // _SUBLANES, b, half, _SUBLANES, dc), lambda j, c: (j, 0, 0, 0, c))],
        out_specs=pl.BlockSpec((rows // _SUBLANES, jb, _SUBLANES, dc), lambda j, c: (0, j, 0, c)),
        out_shape=jax.ShapeDtypeStruct((rows // _SUBLANES, n2, _SUBLANES, d), _F32),
        compiler_params=_params("parallel", "parallel"),
        name="fft_outer_fwd",
    )(m, x5)


def _mid_kernel(*refs, kb, nl):
    ar, ai, kr, ki = (refs[t * nl:(t + 1) * nl] for t in range(4))
    f2r_ref, f2i_ref, twr_ref, twi_ref, b_ref, m2_ref, m2t_ref = refs[4 * nl:]
    i = pl.program_id(0)

    @pl.when(pl.program_id(1) == 0)
    def _():
        f2r = f2r_ref[...]
        f2i = f2i_ref[...]
        for j in range(kb):
            k1 = i * kb + j
            tr = twr_ref[pl.ds(k1, 1), :]
            ti = twi_ref[pl.ds(k1, 1), :]
            cr = f2r * tr - f2i * ti
            ci = f2r * ti + f2i * tr
            m2 = jnp.concatenate([jnp.concatenate([cr, -ci], axis=1),
                                  jnp.concatenate([ci, cr], axis=1)], axis=0)
            m2_ref[j] = m2.astype(_BF16)
            m2t_ref[j] = m2.T.astype(_BF16)

    for j in range(kb):
        pick = lambda rs: jnp.concatenate([_tile_row(r.at[j // _SUBLANES], j % _SUBLANES) for r in rs], axis=1)
        a = jnp.concatenate([pick(ar), pick(ai)], axis=0).astype(_BF16)
        ak = jnp.concatenate([pick(kr), pick(ki)], axis=0).astype(_BF16)
        x = _dot(m2_ref[j], a)
        kf = _dot(m2_ref[j], ak)
        xr, xi = x[:_N2], x[_N2:]
        kfr, kfi = kf[:_N2], kf[_N2:]
        y = jnp.concatenate([xr * kfr - xi * kfi, xr * kfi + xi * kfr], axis=0).astype(_BF16)
        bt = _dot(m2t_ref[j], y)
        b_ref[:, 0, j] = _tiles(bt[:_N2])
        b_ref[:, 1, j] = _tiles(bt[_N2:])


def _mid(a4, ak5, order, c, kb, nl):
    groups, n2, _, d = a4.shape
    n1 = groups * _SUBLANES // 2
    nk = n1 // kb
    kg = kb // _SUBLANES

    def chunks(block, index):
        return [pl.BlockSpec(block + (_LANES,), functools.partial(index, l)) for l in range(nl)]

    sig = (kg, n2, _SUBLANES)
    in_specs = (chunks(sig, lambda l, i, j: (i, 0, 0, nl * j + l))
                + chunks(sig, lambda l, i, j: (nk + i, 0, 0, nl * j + l))
                + chunks((None,) + sig, lambda l, i, j: (order, i, 0, 0, nl * j + l))
                + chunks((None,) + sig, lambda l, i, j: (order, nk + i, 0, 0, nl * j + l))
                + [_const_spec(c[name].shape) for name in ("f2r", "f2i", "twr", "twi")])
    return pl.pallas_call(
        functools.partial(_mid_kernel, kb=kb, nl=nl),
        grid=(nk, d // (nl * _LANES)),
        in_specs=in_specs,
        out_specs=pl.BlockSpec((n2 // _SUBLANES, 2, kb, _SUBLANES, nl * _LANES), lambda i, j: (0, 0, i, 0, j)),
        out_shape=jax.ShapeDtypeStruct((n2 // _SUBLANES, 2, n1, _SUBLANES, d), _F32),
        scratch_shapes=[pltpu.VMEM((kb, 2 * n2, 2 * n2), _BF16), pltpu.VMEM((kb, 2 * n2, 2 * n2), _BF16)],
        compiler_params=_params("parallel", "arbitrary"),
        name="fft_mid",
    )(*([a4] * (2 * nl) + [ak5] * (2 * nl)), c["f2r"], c["f2i"], c["twr"], c["twi"])


def _outer_inv_kernel(minv_ref, mfwd_ref, b_ref, z_ref, gate_ref, bias_ref, zo_ref, ao_ref, *, jb, z_slabs):
    dc = b_ref.shape[-1]
    pair = lambda f, jj: jnp.concatenate([f(jj), f(jj + 1)], axis=1)
    row_of = lambda ref: (lambda q: _tile_row(ref.at[q // _SUBLANES], q % _SUBLANES))
    bias = jnp.concatenate([bias_ref[...], bias_ref[...]], axis=1)
    for jj in range(0, jb, 2):
        bj = pair(row_of(b_ref), jj).astype(_BF16)
        y = _dot(minv_ref[...], bj)
        zj = pair((lambda q: z_ref[q]) if z_slabs else row_of(z_ref), jj)
        zn = pair(row_of(gate_ref), jj) * (y + bias * zj)
        zo_ref[jj] = zn[:, :dc]
        zo_ref[jj + 1] = zn[:, dc:]
        if ao_ref is not None:
            res = _dot(mfwd_ref[...], zn.astype(_BF16))
            ao_ref[:, jj] = _tiles(res[:, :dc])
            ao_ref[:, jj + 1] = _tiles(res[:, dc:])


def _outer_inv_last_kernel(minv_ref, b_ref, z_ref, gate_ref, bias_ref, zo_ref, *, jb, z_slabs):
    _outer_inv_kernel(minv_ref, None, b_ref, z_ref, gate_ref, bias_ref, zo_ref, None, jb=jb, z_slabs=z_slabs)


def _outer_inv(minv, mfwd, b5, z, gate5, bias, jb, dc):
    ng, b, half, _, d = gate5.shape
    n2 = ng * _SUBLANES
    jg = jb // _SUBLANES
    z_slabs = z.ndim == 3
    gspec = pl.BlockSpec((jg, b, half, _SUBLANES, dc), lambda j, c: (j, 0, 0, 0, c))
    zslab = pl.BlockSpec((jb, b * half, dc), lambda j, c: (j, 0, c))
    common = [pl.BlockSpec((jg,) + b5.shape[1:4] + (dc,), lambda j, c: (j, 0, 0, 0, c)),
              zslab if z_slabs else gspec, gspec, pl.BlockSpec((1, dc), lambda j, c: (0, c))]
    zshape = jax.ShapeDtypeStruct((n2, b * half, d), _F32)
    if mfwd is None:
        return pl.pallas_call(
            functools.partial(_outer_inv_last_kernel, jb=jb, z_slabs=z_slabs),
            grid=(n2 // jb, d // dc),
            in_specs=[_const_spec(minv.shape)] + common,
            out_specs=zslab,
            out_shape=zshape,
            compiler_params=_params("parallel", "parallel"),
            name="fft_outer_inv_last",
        )(minv, b5, z, gate5, bias)
    rows = mfwd.shape[0]
    return pl.pallas_call(
        functools.partial(_outer_inv_kernel, jb=jb, z_slabs=z_slabs),
        grid=(n2 // jb, d // dc),
        in_specs=[_const_spec(minv.shape), _const_spec(mfwd.shape)] + common,
        out_specs=[zslab, pl.BlockSpec((rows // _SUBLANES, jb, _SUBLANES, dc), lambda j, c: (0, j, 0, c))],
        out_shape=[zshape, jax.ShapeDtypeStruct((rows // _SUBLANES, n2, _SUBLANES, d), _F32)],
        compiler_params=_params("parallel", "parallel"),
        name="fft_outer_inv_fwd",
    )(minv, mfwd, b5, z, gate5, bias)


def _pick(n, target):
    t = min(n, target)
    while n % t:
        t //= 2
    return t


def _hyena_mixer(x2d, batch, seq, w_in, w_short, fw1, fb1, fw2, fb2, fw3, fb3, freq, fw_out, d_bias):
    assert batch == 2, "batch 0 / batch 1 are packed as real / imaginary parts of one FFT"
    t, d = x2d.shape
    n1 = 2 * seq // _N2
    half = n1 // 2
    hf = fw1.shape[1]
    tm = _pick(seq, 512)
    v, g1, g2 = _hyena_in(x2d, w_in.astype(_BF16), w_short, seq, tm)

    c = _dft_consts(n1)
    fwd_c = c["fwd_c"].astype(_BF16)
    fwd_r = c["fwd_r"].astype(_BF16)
    inv_c = c["inv_c"].astype(_BF16)
    jb = 2 * _SUBLANES
    dc = _LANES

    fl = jnp.concatenate([jnp.zeros((1,), _F32),
                          jnp.linspace(1e-4, _HY_BANDS - 1, _HY_BANDS, dtype=_F32),
                          jnp.linspace(1e-4, _HY_BANDS - 1, _HY_BANDS, dtype=_F32),
                          jnp.zeros((_LANES - _HY_EMB,), _F32)])[None, :]
    w1p = jnp.zeros((_LANES, hf), _F32).at[:_HY_EMB].set(fw1).astype(_BF16)
    deltas = jnp.abs(jnp.linspace(math.log(_HY_FAST) / _HY_TARGET, math.log(_HY_SLOW) / _HY_TARGET,
                                  d, dtype=_F32))[None, :]
    pad = lambda a, axis: jnp.pad(a, [(0, _LANES - a.shape[ax]) if ax == axis else (0, 0) for ax in range(a.ndim)])
    h3 = _filter_mlp(fl, w1p, fb1[None, :], fw2.astype(_BF16), fb2[None, :], pad(fw3, 1).astype(_BF16),
                     pad(fb3[None, :], 1), freq[None, :], pad(freq[None, :], 1), seq, _pick(2 * seq, 1024))
    wo4 = pad(jnp.transpose(fw_out.reshape(hf, 2, 2, d), (2, 1, 0, 3)), 2).astype(_BF16)
    ak = _filter_fft(h3, wo4, deltas, fwd_r, seq, jb, dc)

    kb = _SUBLANES
    nl = _pick(d // _LANES, 4)
    z = v
    gates = (g1, g2)
    a = _outer_fwd(fwd_c, z, jb, dc)
    for o in range(2):
        bsp = _mid(a, ak, o, c, kb, nl)
        bias = d_bias[o][None, :]
        if o == 0:
            z, a = _outer_inv(inv_c, fwd_c, bsp, z, gates[o], bias, jb, dc)
        else:
            z = _outer_inv(inv_c, None, bsp, z, gates[o], bias, jb, dc)
    return jnp.transpose(z.reshape(_N2, batch, half, d), (1, 2, 0, 3)).reshape(t, d)


def _mla_weights(w_in, w_uq, w_ukv):
    d = w_in.shape[0]
    r0 = _Q_LORA + _KV_LORA
    hr = _ROPE // 2
    zpad = jnp.zeros((d, 64), w_in.dtype)
    win = jnp.concatenate([w_in, zpad, w_in[:, r0 + hr:r0 + _ROPE], w_in[:, r0:r0 + hr], zpad], axis=1)
    wq3 = w_uq.reshape(_Q_LORA, _HEADS, _QK)
    wq_n = wq3[:, :, :_NOPE].reshape(_Q_LORA, _HEADS * _NOPE)
    wq_r = wq3[:, :, _NOPE:].reshape(_Q_LORA, _HEADS * _ROPE)
    wq_s = jnp.concatenate([wq3[:, :, _NOPE + hr:], wq3[:, :, _NOPE:_NOPE + hr]], axis=2)
    wqt = jnp.concatenate([wq_n, wq_r, wq_s.reshape(_Q_LORA, _HEADS * _ROPE)], axis=1).T
    wkv3 = w_ukv.reshape(_KV_LORA, _HEADS, _NOPE + _VDIM)
    wk = wkv3[:, :, :_NOPE].reshape(_KV_LORA, _HEADS * _NOPE)
    wvt = wkv3[:, :, _NOPE:].reshape(_KV_LORA, _HEADS * _VDIM).T
    return win.astype(_BF16), wqt.astype(_BF16), wk.astype(_BF16), wvt.astype(_BF16)


def _mla_mixer(x2d, positions, batch, seq, w_in, g_q, w_uq, g_kv, w_ukv):
    win, wqt, wk, wvt = _mla_weights(w_in, w_uq, w_ukv)
    inv = 1.0 / (_ROPE_THETA ** (jnp.arange(0, _ROPE, 2, dtype=_F32) / _ROPE))
    tm = _pick(seq, 512)
    invf = jnp.tile(inv, 2 * _LANES // _ROPE)[None, :]
    invb = jnp.broadcast_to(jnp.tile(inv, 2)[:, None], (_ROPE, tm))
    qt, k, vt = _mla_proj(x2d, positions.reshape(-1, 1), positions.reshape(1, -1), win,
                          g_q[None, :], g_kv[None, :], wqt, wk, wvt, invf, invb, batch, seq, tm)
    o = _attention(qt, k, vt, _pick(seq, 1024), _pick(seq // _KSUB, 32))
    return o.reshape(batch * seq, _HEADS * _VDIM)


def kernel(x, positions, mla_w_in, mla_g_q, mla_w_uq, mla_g_kv, mla_w_ukv, mla_w_o, hy_w_in, hy_w_short, hy_fw1, hy_fb1, hy_fw2, hy_fb2, hy_fw3, hy_fb3, hy_freq, hy_fw_out, hy_d_bias, hy_w_o, ffn_w_up, ffn_w_conv, ffn_w_down, ln1_g, ln1_b, ln2_g, ln2_b):
    batch, seq, d = x.shape
    x2d = x.reshape(batch * seq, d)
    tm = _pick(seq, 512)
    for i in range(_DEPTH):
        j = i // 2
        if i % 2 == 0:
            a = _mla_mixer(x2d, positions, batch, seq, mla_w_in[j], mla_g_q[j], mla_w_uq[j],
                           mla_g_kv[j], mla_w_ukv[j])
            w_o = mla_w_o[j]
        else:
            a = _hyena_mixer(x2d, batch, seq, hy_w_in[j], hy_w_short[j], hy_fw1[j], hy_fb1[j], hy_fw2[j],
                             hy_fb2[j], hy_fw3[j], hy_fb3[j], hy_freq[j], hy_fw_out[j], hy_d_bias[j])
            w_o = hy_w_o[j]
        x2d = _proj_ln(a, x2d, w_o.astype(_BF16), ln1_g[i][None, :], ln1_b[i][None, :], tm)
        x2d = _conv_ffn(x2d, ffn_w_up[i].astype(_BF16), ffn_w_conv[i], ffn_w_down[i].astype(_BF16),
                        ln2_g[i][None, :], ln2_b[i][None, :], seq, tm)
    return x2d.reshape(batch, seq, d)
```

```python
import functools
import math

import numpy as np
import jax
import jax.numpy as jnp
from jax import lax
from jax.experimental import pallas as pl
from jax.experimental.pallas import tpu as pltpu

_F32 = jnp.float32
_BF16 = jnp.bfloat16

_HEADS = 8
_NOPE = 128
_ROPE = 64
_VDIM = 128
_QK = _NOPE + _ROPE
_Q_LORA = 384
_KV_LORA = 256
_ROPE_THETA = 10000.0
_HY_EMB = 33
_HY_BANDS = (_HY_EMB - 1) // 2
_HY_FAST, _HY_SLOW, _HY_TARGET = 0.3, 1.5, 1e-2
_NORM_EPS = 1e-5
_RMS_EPS = 1e-6
_DEPTH = 2
_ALPHA = (2.0 * _DEPTH) ** 0.25

_LANES = 128
_SUBLANES = 8
_VMEM_LIMIT = 56 * 1024 * 1024

_N2 = 128

_KSUB = 256
_KPAD = 16


def _params(*sem, flags=None):
    return pltpu.CompilerParams(dimension_semantics=sem, vmem_limit_bytes=_VMEM_LIMIT, flags=flags)


def _const_spec(shape):
    nd = len(shape)
    return pl.BlockSpec(shape, lambda *_: (0,) * nd, pipeline_mode=pl.Buffered(1))


def _dot(a, b):
    return jnp.dot(a, b, preferred_element_type=_F32)


def _layer_norm(y, g, b):
    mu = jnp.mean(y, axis=-1, keepdims=True)
    yc = y - mu
    var = jnp.mean(yc * yc, axis=-1, keepdims=True)
    return yc * lax.rsqrt(var + _NORM_EPS) * g + b


def _mla_proj_kernel(x_ref, posc_ref, posr_ref, win_ref, gq_ref, gkv_ref, wqt_ref, wk_ref, wvt_ref,
                     invf_ref, invb_ref, qt_ref, k_ref, vt_ref, *, qscale):
    xb = x_ref[...].astype(_BF16)
    h = _dot(xb, win_ref[...])
    cq = h[:, :_Q_LORA]
    ckv = h[:, _Q_LORA:_Q_LORA + _KV_LORA]
    kr = h[:, 640:704]
    krs = h[:, 768:832]

    def rms(c, g):
        ms = jnp.mean(c * c, axis=-1, keepdims=True)
        return (c * lax.rsqrt(ms + _RMS_EPS) * g).astype(_BF16)

    cqn = rms(cq, gq_ref[...])
    ckvn = rms(ckv, gkv_ref[...])
    nt = (((1,), (1,)), ((), ()))
    qt = lax.dot_general(wqt_ref[...], cqn, nt, preferred_element_type=_F32)
    vt = lax.dot_general(wvt_ref[...], ckvn, nt, preferred_element_type=_F32)
    kn = _dot(ckvn, wk_ref[...])

    ang = posc_ref[...].astype(_F32) * invf_ref[...]
    lane = lax.broadcasted_iota(jnp.int32, (1, _LANES), 1)
    sgn = jnp.where(lane % _ROPE < _ROPE // 2, -1.0, 1.0)
    k_rope = (kr * jnp.cos(ang)[:, :_ROPE] + krs * (jnp.sin(ang) * sgn)[:, :_ROPE]).astype(_BF16)

    angt = invb_ref[...] * posr_ref[...].astype(_F32)
    row = lax.broadcasted_iota(jnp.int32, (_ROPE, 1), 0)
    ct = jnp.cos(angt)
    st = jnp.sin(angt) * jnp.where(row < _ROPE // 2, -1.0, 1.0)
    nh = _HEADS * _NOPE
    nr = _HEADS * _ROPE
    k_pad = (lax.broadcasted_iota(jnp.int32, (x_ref.shape[0], _KPAD), 1) == 0).astype(_BF16)
    for hd in range(_HEADS):
        qt_ref[0, hd, :_NOPE, :] = (qt[hd * _NOPE:(hd + 1) * _NOPE] * qscale).astype(_BF16)
        r = qt[nh + hd * _ROPE:nh + (hd + 1) * _ROPE]
        rs = qt[nh + nr + hd * _ROPE:nh + nr + (hd + 1) * _ROPE]
        qt_ref[0, hd, _NOPE:, :] = ((r * ct + rs * st) * qscale).astype(_BF16)
        k_ref[0, hd, :, :_NOPE] = kn[:, hd * _NOPE:(hd + 1) * _NOPE].astype(_BF16)
        k_ref[0, hd, :, _NOPE:_QK] = k_rope
        k_ref[0, hd, :, _QK:] = k_pad
        for c in range(vt_ref.shape[2]):
            vt_ref[0, hd, c] = vt[hd * _VDIM:(hd + 1) * _VDIM, c * _KSUB:(c + 1) * _KSUB].astype(_BF16)


def _mla_proj(x2d, posc, posr, win, gq, gkv, wqt, wk, wvt, invf, invb, batch, seq, tm):
    t, d = x2d.shape
    spb = seq // tm
    qscale = (_QK ** -0.5) * math.log2(math.e)
    consts = [win, gq, gkv, wqt, wk, wvt, invf, invb]
    return pl.pallas_call(
        functools.partial(_mla_proj_kernel, qscale=qscale),
        grid=(t // tm,),
        in_specs=[
            pl.BlockSpec((tm, d), lambda i: (i, 0)),
            pl.BlockSpec((tm, 1), lambda i: (i, 0)),
            pl.BlockSpec((1, tm), lambda i: (0, i)),
        ] + [_const_spec(c.shape) for c in consts],
        out_specs=[
            pl.BlockSpec((1, _HEADS, _QK, tm), lambda i: (i // spb, 0, 0, i % spb)),
            pl.BlockSpec((1, _HEADS, tm, _QK + _KPAD), lambda i: (i // spb, 0, i % spb, 0)),
            pl.BlockSpec((1, _HEADS, tm // _KSUB, _VDIM, _KSUB), lambda i: (i // spb, 0, i % spb, 0, 0)),
        ],
        out_shape=[
            jax.ShapeDtypeStruct((batch, _HEADS, _QK, seq), _BF16),
            jax.ShapeDtypeStruct((batch, _HEADS, seq, _QK + _KPAD), _BF16),
            jax.ShapeDtypeStruct((batch, _HEADS, seq // _KSUB, _VDIM, _KSUB), _BF16),
        ],
        compiler_params=_params("parallel"),
        name="mla_proj",
    )(x2d, posc, posr, *consts)


_ONES_ROWS = _KPAD
_KW = _QK + _KPAD
_TAU = 20.0


def _attn_kernel(qt_ref, k_ref, vt_ref, o_ref, qx_ref, r_ref, p_ref, acc_ref, *, sub, nsub, group):
    tq = qt_ref.shape[-1]
    ones = (lax.broadcasted_iota(jnp.int32, (_ONES_ROWS, sub), 0) == 0).astype(_BF16)
    row = lax.broadcasted_iota(jnp.int32, (_ONES_ROWS, tq), 0)

    def set_reference(r):
        r_ref[...] = r
        qx_ref[_QK:, :] = jnp.where(row == 0, -r, 0.0).astype(_BF16)

    def shifted_scores(j):
        kj = k_ref[0, 0, pl.ds(pl.multiple_of(j * sub, sub), sub), :]
        return _dot(kj, qx_ref[...])

    def probs(j, slot):
        st = shifted_scores(j)
        p_ref[slot] = jnp.exp2(st).astype(_BF16)
        return jnp.max(st, axis=0, keepdims=True)

    def accumulate(j, slot):
        vte = jnp.concatenate([vt_ref[0, 0, j], ones], axis=0)
        acc_ref[...] += _dot(vte, p_ref[slot])

    qx_ref[:_QK, :] = qt_ref[0, 0]
    set_reference(jnp.zeros((1, tq), _F32))
    acc_ref[...] = jnp.zeros(acc_ref.shape, _F32)
    first = jnp.max(shifted_scores(0), axis=0, keepdims=True)
    set_reference(first.astype(_BF16).astype(_F32))

    def body(t, carry):
        base = t * group
        cm = probs(base, 0)
        for g in range(1, group):
            cm = jnp.maximum(cm, probs(base + g, g))

        @pl.when(jnp.max(cm) > _TAU)
        def _():
            r_old = r_ref[...]
            r_new = (r_old + jnp.maximum(cm, 0.0)).astype(_BF16).astype(_F32)
            acc_ref[...] = acc_ref[...] * jnp.exp2(r_old - r_new)
            set_reference(r_new)
            for g in range(group):
                probs(base + g, g)

        for g in range(group):
            accumulate(base + g, g)
        return carry

    lax.fori_loop(0, nsub // group, body, 0)
    acc = acc_ref[...]
    o = acc[:_VDIM] / acc[_VDIM:_VDIM + 1]
    o_ref[0] = o.T.astype(o_ref.dtype)


def _attention(qt, k, vt5, tq, group):
    b, h, s, kw = k.shape
    nsub, sub = vt5.shape[2], vt5.shape[4]
    assert nsub % group == 0
    return pl.pallas_call(
        functools.partial(_attn_kernel, sub=sub, nsub=nsub, group=group),
        grid=(b, h, s // tq),
        in_specs=[
            pl.BlockSpec((1, 1, _QK, tq), lambda bi, hi, qi: (bi, hi, 0, qi)),
            pl.BlockSpec((1, 1, s, kw), lambda bi, hi, qi: (bi, hi, 0, 0)),
            pl.BlockSpec((1, 1, nsub, _VDIM, sub), lambda bi, hi, qi: (bi, hi, 0, 0, 0)),
        ],
        out_specs=pl.BlockSpec((1, tq, _VDIM), lambda bi, hi, qi: (bi, qi, hi)),
        out_shape=jax.ShapeDtypeStruct((b, s, h * _VDIM), _BF16),
        scratch_shapes=[
            pltpu.VMEM((kw, tq), _BF16),
            pltpu.VMEM((1, tq), _F32),
            pltpu.VMEM((group, sub, tq), _BF16),
            pltpu.VMEM((_VDIM + _ONES_ROWS, tq), _F32),
        ],
        compiler_params=_params("parallel", "parallel", "parallel"),
        name="mla_attention",
    )(qt, k, vt5)


def _proj_ln_kernel(a_ref, x_ref, w_ref, g_ref, b_ref, o_ref):
    m = _dot(a_ref[...].astype(_BF16), w_ref[...])
    o_ref[...] = _layer_norm(_ALPHA * x_ref[...] + m, g_ref[...], b_ref[...])


def _proj_ln(a2d, x2d, w, g, b, tm):
    t, d = x2d.shape
    ka = a2d.shape[1]
    return pl.pallas_call(
        _proj_ln_kernel,
        grid=(t // tm,),
        in_specs=[
            pl.BlockSpec((tm, ka), lambda i: (i, 0)),
            pl.BlockSpec((tm, d), lambda i: (i, 0)),
            _const_spec(w.shape), _const_spec(g.shape), _const_spec(b.shape),
        ],
        out_specs=pl.BlockSpec((tm, d), lambda i: (i, 0)),
        out_shape=jax.ShapeDtypeStruct((t, d), _F32),
        compiler_params=_params("parallel"),
        name="proj_ln",
    )(a2d, x2d, w, g, b)


def _halo_specs(tm, d, nrows):
    r = tm // _SUBLANES
    last = nrows // _SUBLANES - 1
    return [
        pl.BlockSpec((_SUBLANES, d), lambda i: (jnp.maximum(i * r - 1, 0), 0)),
        pl.BlockSpec((tm, d), lambda i: (i, 0)),
        pl.BlockSpec((_SUBLANES, d), lambda i: (jnp.minimum((i + 1) * r, last), 0)),
    ]


def _assemble_halo(xp_ref, x_ref, xn_ref, xcat_ref, tm, tiles_per_seq):
    i = pl.program_id(0)
    first = (i % tiles_per_seq) == 0
    last = (i % tiles_per_seq) == tiles_per_seq - 1
    xcat_ref[0:_SUBLANES, :] = jnp.where(first, 0.0, xp_ref[...])
    xcat_ref[_SUBLANES:_SUBLANES + tm, :] = x_ref[...]
    xcat_ref[_SUBLANES + tm:, :] = jnp.where(last, 0.0, xn_ref[...])


def _dwconv3(h_ref, wc_ref, tm):
    lo = h_ref[pl.ds(_SUBLANES - 1, tm), :]
    mid = h_ref[pl.ds(_SUBLANES, tm), :]
    hi = h_ref[pl.ds(_SUBLANES + 1, tm), :]
    return lo * wc_ref[0:1, :] + mid * wc_ref[1:2, :] + hi * wc_ref[2:3, :]


def _conv_ffn_kernel(xp_ref, x_ref, xn_ref, wup_ref, wc_ref, wdn_ref, g_ref, b_ref, o_ref,
                     xcat_ref, h_ref, *, tm, tiles_per_seq, dff):
    _assemble_halo(xp_ref, x_ref, xn_ref, xcat_ref, tm, tiles_per_seq)
    h_ref[...] = _dot(xcat_ref[...].astype(_BF16), wup_ref[...])
    hc = _dwconv3(h_ref, wc_ref, tm)
    a = hc[:, :dff]
    gt = hc[:, dff:]
    act = (gt * jax.nn.sigmoid(gt) * a).astype(_BF16)
    f = _dot(act, wdn_ref[...])
    o_ref[...] = _layer_norm(_ALPHA * x_ref[...] + f, g_ref[...], b_ref[...])


def _conv_ffn(x2d, wup, wc, wdn, g, b, seq, tm):
    t, d = x2d.shape
    dff = wdn.shape[0]
    return pl.pallas_call(
        functools.partial(_conv_ffn_kernel, tm=tm, tiles_per_seq=seq // tm, dff=dff),
        grid=(t // tm,),
        in_specs=_halo_specs(tm, d, t) + [
            _const_spec(wup.shape), _const_spec(wc.shape), _const_spec(wdn.shape),
            _const_spec(g.shape), _const_spec(b.shape),
        ],
        out_specs=pl.BlockSpec((tm, d), lambda i: (i, 0)),
        out_shape=jax.ShapeDtypeStruct((t, d), _F32),
        scratch_shapes=[
            pltpu.VMEM((tm + 2 * _SUBLANES, d), _F32),
            pltpu.VMEM((tm + 2 * _SUBLANES, 2 * dff), _F32),
        ],
        compiler_params=_params("parallel"),
        name="conv_ffn",
    )(x2d, x2d, x2d, wup, wc, wdn, g, b)


def _hyena_in_kernel(xp_ref, x_ref, xn_ref, win_ref, wc_ref, v_ref, g1_ref, g2_ref,
                     xcat_ref, h_ref, *, tm, tiles_per_seq, d):
    _assemble_halo(xp_ref, x_ref, xn_ref, xcat_ref, tm, tiles_per_seq)
    h_ref[...] = _dot(xcat_ref[...].astype(_BF16), win_ref[...])
    u = _dwconv3(h_ref, wc_ref, tm)
    for a in range(tm // _N2):
        rows = slice(a * _N2, (a + 1) * _N2)
        for o_ref, c in ((v_ref, 0), (g1_ref, 1), (g2_ref, 2)):
            o_ref[:, 0, a] = u[rows, c * d:(c + 1) * d].reshape(_N2 // _SUBLANES, _SUBLANES, d)


def _hyena_in(x2d, win, wc, seq, tm):
    t, d = x2d.shape
    spb = seq // tm
    ng = _N2 // _SUBLANES
    out = jax.ShapeDtypeStruct((ng, t // seq, seq // _N2, _SUBLANES, d), _F32)
    ospec = pl.BlockSpec((ng, 1, tm // _N2, _SUBLANES, d), lambda i: (0, i // spb, i % spb, 0, 0))
    return pl.pallas_call(
        functools.partial(_hyena_in_kernel, tm=tm, tiles_per_seq=seq // tm, d=d),
        grid=(t // tm,),
        in_specs=_halo_specs(tm, d, t) + [_const_spec(win.shape), _const_spec(wc.shape)],
        out_specs=[ospec, ospec, ospec],
        out_shape=[out, out, out],
        scratch_shapes=[
            pltpu.VMEM((tm + 2 * _SUBLANES, d), _F32),
            pltpu.VMEM((tm + 2 * _SUBLANES, 3 * d), _F32),
        ],
        compiler_params=_params("parallel"),
        name="hyena_in",
    )(x2d, x2d, x2d, win, wc)


def _dft_consts(n1):
    n = n1 * _N2
    k = np.arange(n1)
    ang1 = -2.0 * np.pi * ((k[:, None] * k[None, :]) % n1) / n1
    f1r, f1i = np.cos(ang1), np.sin(ang1)
    half = n1 // 2
    fwd_c = np.block([[f1r[:, :half], -f1i[:, :half]], [f1i[:, :half], f1r[:, :half]]])
    fwd_r = np.concatenate([f1r, f1i], axis=0)
    inv_c = np.block([[f1r[:half], f1i[:half]], [-f1i[:half], f1r[:half]]]) / n
    k2 = np.arange(_N2)
    ang2 = -2.0 * np.pi * ((k2[:, None] * k2[None, :]) % _N2) / _N2
    angt = -2.0 * np.pi * ((k[:, None] * k2[None, :]) % n) / n
    f32 = lambda a: jnp.asarray(a.astype(np.float32))
    return dict(fwd_c=f32(fwd_c), fwd_r=f32(fwd_r), inv_c=f32(inv_c),
                f2r=f32(np.cos(ang2)), f2i=f32(np.sin(ang2)),
                twr=f32(np.cos(angt)), twi=f32(np.sin(angt)))


def _tile_row(ref, r):
    flat = ref.reshape(math.prod(ref.shape[:-1]), ref.shape[-1])
    return flat[pl.ds(r, flat.shape[0] // _SUBLANES, stride=_SUBLANES), :]


def _tiles(x):
    return x.reshape(x.shape[0] // _SUBLANES, _SUBLANES, x.shape[1])


def _filter_mlp_kernel(fl_ref, w1_ref, b1_ref, w2_ref, b2_ref, w3_ref, b3_ref, fr_ref, h_ref, *, rows, length):
    hn2 = _N2 // 2
    q = lax.broadcasted_iota(jnp.int32, (rows, 1), 0)
    r_lo = _N2 * (pl.program_id(0) * (rows // hn2) + q // hn2) + q % hn2
    lane = lax.broadcasted_iota(jnp.int32, (rows, _LANES), 1)
    phase = jnp.where(lane <= _HY_BANDS, 0.5 * math.pi, math.pi)

    def features(r):
        lag = jnp.where(r < length, r, 2 * length - r).astype(_F32)
        a = (lag * (2.0 * math.pi / length)) * fl_ref[...]
        z = jnp.where(lane == 0, lag / (length - 1.0), jnp.where(lane <= 2 * _HY_BANDS, jnp.sin(a + phase), 0.0))
        return z.astype(_BF16)

    fr = fr_ref[...]
    h = jnp.sin(fr * (_dot(features(r_lo), w1_ref[0]) + _dot(features(r_lo + hn2), w1_ref[1]) + b1_ref[...]))
    h = jnp.sin(fr * (_dot(h.astype(_BF16), w2_ref[...]) + b2_ref[...]))
    h = jnp.sin(fr * (_dot(h.astype(_BF16), w3_ref[...]) + b3_ref[...]))
    for a in range(rows // hn2):
        h_ref[:, a] = h[a * hn2:(a + 1) * hn2].reshape(hn2 // _SUBLANES, _SUBLANES, _LANES)


def _filter_mlp(fl, w1, b1, w2, b2, w3, b3, fr, length, rows):
    consts = [fl, w1, b1, w2, b2, w3, b3, fr]
    hn2 = _N2 // 2
    n1 = 2 * length // _N2
    return pl.pallas_call(
        functools.partial(_filter_mlp_kernel, rows=rows, length=length),
        grid=(n1 * hn2 // rows,),
        in_specs=[_const_spec(c.shape) for c in consts],
        out_specs=pl.BlockSpec((hn2 // _SUBLANES, rows // hn2, _SUBLANES, _LANES), lambda i: (0, i, 0, 0)),
        out_shape=jax.ShapeDtypeStruct((hn2 // _SUBLANES, n1, _SUBLANES, _LANES), _F32),
        compiler_params=_params("parallel"),
        name="hyena_filter_mlp",
    )(*consts)


def _filter_fft_kernel(h_ref, wo_ref, dl_ref, mf_ref, ak_ref, *, jb, length):
    n1 = h_ref.shape[1]
    half = n1 // 2
    j0 = pl.program_id(0) * jb
    row = lax.broadcasted_iota(jnp.int32, (n1, 1), 0)
    dc = dl_ref.shape[1]

    def filter_rows(jj, o):
        r = _N2 * row + (j0 + jj)
        lag = jnp.where(r < length, r, 2 * length - r).astype(_F32)
        decay = jnp.exp(-(lag / (length - 1.0)) * dl_ref[...])
        hb = _tile_row(h_ref.at[jj // _SUBLANES], jj % _SUBLANES).astype(_BF16)
        out = jnp.concatenate([_dot(hb[:half], wo_ref[0, o]), _dot(hb[half:], wo_ref[1, o])], axis=0)
        lag0 = _dot(hb[0:_SUBLANES], wo_ref[1, o])[0:1]
        ko = jnp.where(r != length, out * decay, 0.0) + jnp.where(r == 0, lag0 * decay[0:1], 0.0)
        return ko.astype(_BF16)

    for jj in range(0, jb, 2):
        for o in range(2):
            res = _dot(mf_ref[...], jnp.concatenate([filter_rows(jj, o), filter_rows(jj + 1, o)], axis=1))
            ak_ref[o, :, jj] = _tiles(res[:, :dc])
            ak_ref[o, :, jj + 1] = _tiles(res[:, dc:])


def _filter_fft(h4, wo4, dl, mf, length, jb, dc):
    ng, n1, _, hf = h4.shape
    n2 = 2 * ng * _SUBLANES
    nb = n2 // 2 // jb
    d = dl.shape[1]
    rows = mf.shape[0]
    return pl.pallas_call(
        functools.partial(_filter_fft_kernel, jb=jb, length=length),
        grid=(n2 // jb, d // dc),
        in_specs=[pl.BlockSpec((jb // _SUBLANES, n1, _SUBLANES, hf), lambda j, c: (j % nb, 0, 0, 0)),
                  pl.BlockSpec((None, 2, 2, hf, dc), lambda j, c: (j // nb, 0, 0, 0, c)),
                  pl.BlockSpec((1, dc), lambda j, c: (0, c)),
                  _const_spec(mf.shape)],
        out_specs=pl.BlockSpec((2, rows // _SUBLANES, jb, _SUBLANES, dc), lambda j, c: (0, 0, j, 0, c)),
        out_shape=jax.ShapeDtypeStruct((2, rows // _SUBLANES, n2, _SUBLANES, d), _F32),
        compiler_params=_params("parallel", "parallel"),
        name="hyena_filter_fft",
    )(h4, wo4, dl, mf)


def _outer_fwd_kernel(m_ref, x_ref, o_ref, *, jb):
    dc = x_ref.shape[-1]
    pick = lambda jj: _tile_row(x_ref.at[jj // _SUBLANES], jj % _SUBLANES)
    for jj in range(0, jb, 2):
        res = _dot(m_ref[...], jnp.concatenate([pick(jj), pick(jj + 1)], axis=1).astype(_BF16))
        o_ref[:, jj] = _tiles(res[:, :dc])
        o_ref[:, jj + 1] = _tiles(res[:, dc:])


def _outer_fwd(m, x5, jb, dc):
    ng, b, half, _, d = x5.shape
    n2 = ng * _SUBLANES
    rows = m.shape[0]
    return pl.pallas_call(
        functools.partial(_outer_fwd_kernel, jb=jb),
        grid=(n2 // jb, d // dc),
        in_specs=[_const_spec(m.shape),
                  pl.BlockSpec((jb // _SUBLANES, b, half, _SUBLANES, dc), lambda j, c: (j, 0, 0, 0, c))],
        out_specs=pl.BlockSpec((rows // _SUBLANES, jb, _SUBLANES, dc), lambda j, c: (0, j, 0, c)),
        out_shape=jax.ShapeDtypeStruct((rows // _SUBLANES, n2, _SUBLANES, d), _F32),
        compiler_params=_params("parallel", "parallel"),
        name="fft_outer_fwd",
    )(m, x5)


def _mid_kernel(*refs, kb, nl):
    ar, ai, kr, ki = (refs[t * nl:(t + 1) * nl] for t in range(4))
    f2r_ref, f2i_ref, twr_ref, twi_ref, b_ref, m2_ref, m2t_ref = refs[4 * nl:]
    i = pl.program_id(0)

    @pl.when(pl.program_id(1) == 0)
    def _():
        f2r = f2r_ref[...]
        f2i = f2i_ref[...]
        for j in range(kb):
            k1 = i * kb + j
            tr = twr_ref[pl.ds(k1, 1), :]
            ti = twi_ref[pl.ds(k1, 1), :]
            cr = f2r * tr - f2i * ti
            ci = f2r * ti + f2i * tr
            m2 = jnp.concatenate([jnp.concatenate([cr, -ci], axis=1),
                                  jnp.concatenate([ci, cr], axis=1)], axis=0)
            m2_ref[j] = m2.astype(_BF16)
            m2t_ref[j] = m2.T.astype(_BF16)

    for j in range(kb):
        pick = lambda rs: jnp.concatenate([_tile_row(r.at[j // _SUBLANES], j % _SUBLANES) for r in rs], axis=1)
        a = jnp.concatenate([pick(ar), pick(ai)], axis=0).astype(_BF16)
        ak = jnp.concatenate([pick(kr), pick(ki)], axis=0).astype(_BF16)
        x = _dot(m2_ref[j], a)
        kf = _dot(m2_ref[j], ak)
        xr, xi = x[:_N2], x[_N2:]
        kfr, kfi = kf[:_N2], kf[_N2:]
        y = jnp.concatenate([xr * kfr - xi * kfi, xr * kfi + xi * kfr], axis=0).astype(_BF16)
        bt = _dot(m2t_ref[j], y)
        b_ref[:, 0, j] = _tiles(bt[:_N2])
        b_ref[:, 1, j] = _tiles(bt[_N2:])


def _mid(a4, ak5, order, c, kb, nl):
    groups, n2, _, d = a4.shape
    n1 = groups * _SUBLANES // 2
    nk = n1 // kb
    kg = kb // _SUBLANES

    def chunks(block, index):
        return [pl.BlockSpec(block + (_LANES,), functools.partial(index, l)) for l in range(nl)]

    sig = (kg, n2, _SUBLANES)
    in_specs = (chunks(sig, lambda l, i, j: (i, 0, 0, nl * j + l))
                + chunks(sig, lambda l, i, j: (nk + i, 0, 0, nl * j + l))
                + chunks((None,) + sig, lambda l, i, j: (order, i, 0, 0, nl * j + l))
                + chunks((None,) + sig, lambda l, i, j: (order, nk + i, 0, 0, nl * j + l))
                + [_const_spec(c[name].shape) for name in ("f2r", "f2i", "twr", "twi")])
    return pl.pallas_call(
        functools.partial(_mid_kernel, kb=kb, nl=nl),
        grid=(nk, d // (nl * _LANES)),
        in_specs=in_specs,
        out_specs=pl.BlockSpec((n2 // _SUBLANES, 2, kb, _SUBLANES, nl * _LANES), lambda i, j: (0, 0, i, 0, j)),
        out_shape=jax.ShapeDtypeStruct((n2 // _SUBLANES, 2, n1, _SUBLANES, d), _F32),
        scratch_shapes=[pltpu.VMEM((kb, 2 * n2, 2 * n2), _BF16), pltpu.VMEM((kb, 2 * n2, 2 * n2), _BF16)],
        compiler_params=_params("parallel", "arbitrary"),
        name="fft_mid",
    )(*([a4] * (2 * nl) + [ak5] * (2 * nl)), c["f2r"], c["f2i"], c["twr"], c["twi"])


def _outer_inv_kernel(minv_ref, mfwd_ref, b_ref, z_ref, gate_ref, bias_ref, zo_ref, ao_ref, *, jb, z_slabs):
    dc = b_ref.shape[-1]
    pair = lambda f, jj: jnp.concatenate([f(jj), f(jj + 1)], axis=1)
    row_of = lambda ref: (lambda q: _tile_row(ref.at[q // _SUBLANES], q % _SUBLANES))
    bias = jnp.concatenate([bias_ref[...], bias_ref[...]], axis=1)
    for jj in range(0, jb, 2):
        bj = pair(row_of(b_ref), jj).astype(_BF16)
        y = _dot(minv_ref[...], bj)
        zj = pair((lambda q: z_ref[q]) if z_slabs else row_of(z_ref), jj)
        zn = pair(row_of(gate_ref), jj) * (y + bias * zj)
        zo_ref[jj] = zn[:, :dc]
        zo_ref[jj + 1] = zn[:, dc:]
        if ao_ref is not None:
            res = _dot(mfwd_ref[...], zn.astype(_BF16))
            ao_ref[:, jj] = _tiles(res[:, :dc])
            ao_ref[:, jj + 1] = _tiles(res[:, dc:])


def _outer_inv_last_kernel(minv_ref, b_ref, z_ref, gate_ref, bias_ref, zo_ref, *, jb, z_slabs):
    _outer_inv_kernel(minv_ref, None, b_ref, z_ref, gate_ref, bias_ref, zo_ref, None, jb=jb, z_slabs=z_slabs)


def _outer_inv(minv, mfwd, b5, z, gate5, bias, jb, dc):
    ng, b, half, _, d = gate5.shape
    n2 = ng * _SUBLANES
    jg = jb // _SUBLANES
    z_slabs = z.ndim == 3
    gspec = pl.BlockSpec((jg, b, half, _SUBLANES, dc), lambda j, c: (j, 0, 0, 0, c))
    zslab = pl.BlockSpec((jb, b * half, dc), lambda j, c: (j, 0, c))
    common = [pl.BlockSpec((jg,) + b5.shape[1:4] + (dc,), lambda j, c: (j, 0, 0, 0, c)),
              zslab if z_slabs else gspec, gspec, pl.BlockSpec((1, dc), lambda j, c: (0, c))]
    zshape = jax.ShapeDtypeStruct((n2, b * half, d), _F32)
    if mfwd is None:
        return pl.pallas_call(
            functools.partial(_outer_inv_last_kernel, jb=jb, z_slabs=z_slabs),
            grid=(n2 // jb, d // dc),
            in_specs=[_const_spec(minv.shape)] + common,
            out_specs=zslab,
            out_shape=zshape,
            compiler_params=_params("parallel", "parallel"),
            name="fft_outer_inv_last",
        )(minv, b5, z, gate5, bias)
    rows = mfwd.shape[0]
    return pl.pallas_call(
        functools.partial(_outer_inv_kernel, jb=jb, z_slabs=z_slabs),
        grid=(n2 // jb, d // dc),
        in_specs=[_const_spec(minv.shape), _const_spec(mfwd.shape)] + common,
        out_specs=[zslab, pl.BlockSpec((rows // _SUBLANES, jb, _SUBLANES, dc), lambda j, c: (0, j, 0, c))],
        out_shape=[zshape, jax.ShapeDtypeStruct((rows // _SUBLANES, n2, _SUBLANES, d), _F32)],
        compiler_params=_params("parallel", "parallel"),
        name="fft_outer_inv_fwd",
    )(minv, mfwd, b5, z, gate5, bias)


def _pick(n, target):
    t = min(n, target)
    while n % t:
        t //= 2
    return t


def _hyena_mixer(x2d, batch, seq, w_in, w_short, fw1, fb1, fw2, fb2, fw3, fb3, freq, fw_out, d_bias):
    assert batch == 2, "batch 0 / batch 1 are packed as real / imaginary parts of one FFT"
    t, d = x2d.shape
    n1 = 2 * seq // _N2
    half = n1 // 2
    hf = fw1.shape[1]
    tm = _pick(seq, 512)
    v, g1, g2 = _hyena_in(x2d, w_in.astype(_BF16), w_short, seq, tm)

    c = _dft_consts(n1)
    fwd_c = c["fwd_c"].astype(_BF16)
    fwd_r = c["fwd_r"].astype(_BF16)
    inv_c = c["inv_c"].astype(_BF16)
    jb = 2 * _SUBLANES
    dc = _LANES

    fl = jnp.concatenate([jnp.zeros((1,), _F32),
                          jnp.linspace(1e-4, _HY_BANDS - 1, _HY_BANDS, dtype=_F32),
                          jnp.linspace(1e-4, _HY_BANDS - 1, _HY_BANDS, dtype=_F32),
                          jnp.zeros((_LANES - _HY_EMB,), _F32)])[None, :]
    assert 2 * hf == _LANES
    w1z = jnp.zeros((_LANES, _LANES), _F32)
    w1p = jnp.stack([w1z.at[:_HY_EMB, :hf].set(fw1), w1z.at[:_HY_EMB, hf:].set(fw1)]).astype(_BF16)
    twice = lambda a: jnp.tile(a[None, :], (1, 2))
    blockdiag = lambda w: jnp.zeros((_LANES, _LANES), _F32).at[:hf, :hf].set(w).at[hf:, hf:].set(w).astype(_BF16)
    deltas = jnp.abs(jnp.linspace(math.log(_HY_FAST) / _HY_TARGET, math.log(_HY_SLOW) / _HY_TARGET,
                                  d, dtype=_F32))[None, :]
    h3 = _filter_mlp(fl, w1p, twice(fb1), blockdiag(fw2), twice(fb2), blockdiag(fw3), twice(fb3), twice(freq),
                     seq, _pick(seq, 1024))
    wo = jnp.transpose(fw_out.reshape(hf, 2, 2, d), (2, 1, 0, 3))
    woz = jnp.zeros_like(wo)
    wo5 = jnp.stack([jnp.concatenate([wo, woz], axis=2), jnp.concatenate([woz, wo], axis=2)]).astype(_BF16)
    ak = _filter_fft(h3, wo5, deltas, fwd_r, seq, jb, dc)

    kb = _SUBLANES
    nl = _pick(d // _LANES, 4)
    z = v
    gates = (g1, g2)
    a = _outer_fwd(fwd_c, z, jb, dc)
    for o in range(2):
        bsp = _mid(a, ak, o, c, kb, nl)
        bias = d_bias[o][None, :]
        if o == 0:
            z, a = _outer_inv(inv_c, fwd_c, bsp, z, gates[o], bias, jb, dc)
        else:
            z = _outer_inv(inv_c, None, bsp, z, gates[o], bias, jb, dc)
    return jnp.transpose(z.reshape(_N2, batch, half, d), (1, 2, 0, 3)).reshape(t, d)


def _mla_weights(w_in, w_uq, w_ukv):
    d = w_in.shape[0]
    r0 = _Q_LORA + _KV_LORA
    hr = _ROPE // 2
    zpad = jnp.zeros((d, 64), w_in.dtype)
    win = jnp.concatenate([w_in, zpad, w_in[:, r0 + hr:r0 + _ROPE], w_in[:, r0:r0 + hr], zpad], axis=1)
    wq3 = w_uq.reshape(_Q_LORA, _HEADS, _QK)
    wq_n = wq3[:, :, :_NOPE].reshape(_Q_LORA, _HEADS * _NOPE)
    wq_r = wq3[:, :, _NOPE:].reshape(_Q_LORA, _HEADS * _ROPE)
    wq_s = jnp.concatenate([wq3[:, :, _NOPE + hr:], wq3[:, :, _NOPE:_NOPE + hr]], axis=2)
    wqt = jnp.concatenate([wq_n, wq_r, wq_s.reshape(_Q_LORA, _HEADS * _ROPE)], axis=1).T
    wkv3 = w_ukv.reshape(_KV_LORA, _HEADS, _NOPE + _VDIM)
    wk = wkv3[:, :, :_NOPE].reshape(_KV_LORA, _HEADS * _NOPE)
    wvt = wkv3[:, :, _NOPE:].reshape(_KV_LORA, _HEADS * _VDIM).T
    return win.astype(_BF16), wqt.astype(_BF16), wk.astype(_BF16), wvt.astype(_BF16)


def _mla_mixer(x2d, positions, batch, seq, w_in, g_q, w_uq, g_kv, w_ukv):
    win, wqt, wk, wvt = _mla_weights(w_in, w_uq, w_ukv)
    inv = 1.0 / (_ROPE_THETA ** (jnp.arange(0, _ROPE, 2, dtype=_F32) / _ROPE))
    tm = _pick(seq, 512)
    invf = jnp.tile(inv, 2 * _LANES // _ROPE)[None, :]
    invb = jnp.broadcast_to(jnp.tile(inv, 2)[:, None], (_ROPE, tm))
    qt, k, vt = _mla_proj(x2d, positions.reshape(-1, 1), positions.reshape(1, -1), win,
                          g_q[None, :], g_kv[None, :], wqt, wk, wvt, invf, invb, batch, seq, tm)
    o = _attention(qt, k, vt, _pick(seq, 1024), _pick(seq // _KSUB, 32))
    return o.reshape(batch * seq, _HEADS * _VDIM)


def kernel(x, positions, mla_w_in, mla_g_q, mla_w_uq, mla_g_kv, mla_w_ukv, mla_w_o, hy_w_in, hy_w_short, hy_fw1, hy_fb1, hy_fw2, hy_fb2, hy_fw3, hy_fb3, hy_freq, hy_fw_out, hy_d_bias, hy_w_o, ffn_w_up, ffn_w_conv, ffn_w_down, ln1_g, ln1_b, ln2_g, ln2_b):
    batch, seq, d = x.shape
    x2d = x.reshape(batch * seq, d)
    tm = _pick(seq, 512)
    for i in range(_DEPTH):
        j = i // 2
        if i % 2 == 0:
            a = _mla_mixer(x2d, positions, batch, seq, mla_w_in[j], mla_g_q[j], mla_w_uq[j],
                           mla_g_kv[j], mla_w_ukv[j])
            w_o = mla_w_o[j]
        else:
            a = _hyena_mixer(x2d, batch, seq, hy_w_in[j], hy_w_short[j], hy_fw1[j], hy_fb1[j], hy_fw2[j],
                             hy_fb2[j], hy_fw3[j], hy_fb3[j], hy_freq[j], hy_fw_out[j], hy_d_bias[j])
            w_o = hy_w_o[j]
        x2d = _proj_ln(a, x2d, w_o.astype(_BF16), ln1_g[i][None, :], ln1_b[i][None, :], tm)
        x2d = _conv_ffn(x2d, ffn_w_up[i].astype(_BF16), ffn_w_conv[i], ffn_w_down[i].astype(_BF16),
                        ln2_g[i][None, :], ln2_b[i][None, :], seq, tm)
    return x2d.reshape(batch, seq, d)
```

```python
import functools
import math

import numpy as np
import jax
import jax.numpy as jnp
from jax import lax
from jax.experimental import pallas as pl
from jax.experimental.pallas import tpu as pltpu

_F32 = jnp.float32
_BF16 = jnp.bfloat16

_HEADS = 8
_NOPE = 128
_ROPE = 64
_VDIM = 128
_QK = _NOPE + _ROPE
_Q_LORA = 384
_KV_LORA = 256
_ROPE_THETA = 10000.0
_HY_EMB = 33
_HY_BANDS = (_HY_EMB - 1) // 2
_HY_FAST, _HY_SLOW, _HY_TARGET = 0.3, 1.5, 1e-2
_NORM_EPS = 1e-5
_RMS_EPS = 1e-6
_DEPTH = 2
_ALPHA = (2.0 * _DEPTH) ** 0.25

_LANES = 128
_SUBLANES = 8
_VMEM_LIMIT = 56 * 1024 * 1024

_N2 = 128

_KSUB = 256
_KPAD = 16


def _params(*sem, flags=None):
    return pltpu.CompilerParams(dimension_semantics=sem, vmem_limit_bytes=_VMEM_LIMIT, flags=flags)


def _const_spec(shape):
    nd = len(shape)
    return pl.BlockSpec(shape, lambda *_: (0,) * nd, pipeline_mode=pl.Buffered(1))


def _dot(a, b):
    return jnp.dot(a, b, preferred_element_type=_F32)


def _layer_norm(y, g, b):
    mu = jnp.mean(y, axis=-1, keepdims=True)
    yc = y - mu
    var = jnp.mean(yc * yc, axis=-1, keepdims=True)
    return yc * lax.rsqrt(var + _NORM_EPS) * g + b


def _mla_proj_kernel(x_ref, posc_ref, posr_ref, win_ref, gq_ref, gkv_ref, wqt_ref, wk_ref, wvt_ref,
                     invf_ref, invb_ref, qt_ref, k_ref, vt_ref, *, qscale):
    xb = x_ref[...].astype(_BF16)
    h = _dot(xb, win_ref[...])
    cq = h[:, :_Q_LORA]
    ckv = h[:, _Q_LORA:_Q_LORA + _KV_LORA]
    kr = h[:, 640:704]
    krs = h[:, 768:832]

    def rms(c, g):
        ms = jnp.mean(c * c, axis=-1, keepdims=True)
        return (c * lax.rsqrt(ms + _RMS_EPS) * g).astype(_BF16)

    cqn = rms(cq, gq_ref[...])
    ckvn = rms(ckv, gkv_ref[...])
    nt = (((1,), (1,)), ((), ()))
    qt = lax.dot_general(wqt_ref[...], cqn, nt, preferred_element_type=_F32)
    vt = lax.dot_general(wvt_ref[...], ckvn, nt, preferred_element_type=_F32)
    kn = _dot(ckvn, wk_ref[...])

    ang = posc_ref[...].astype(_F32) * invf_ref[...]
    lane = lax.broadcasted_iota(jnp.int32, (1, _LANES), 1)
    sgn = jnp.where(lane % _ROPE < _ROPE // 2, -1.0, 1.0)
    k_rope = (kr * jnp.cos(ang)[:, :_ROPE] + krs * (jnp.sin(ang) * sgn)[:, :_ROPE]).astype(_BF16)

    angt = invb_ref[...] * posr_ref[...].astype(_F32)
    row = lax.broadcasted_iota(jnp.int32, (_ROPE, 1), 0)
    ct = jnp.cos(angt)
    st = jnp.sin(angt) * jnp.where(row < _ROPE // 2, -1.0, 1.0)
    nh = _HEADS * _NOPE
    nr = _HEADS * _ROPE
    k_pad = (lax.broadcasted_iota(jnp.int32, (x_ref.shape[0], _KPAD), 1) == 0).astype(_BF16)
    for hd in range(_HEADS):
        qt_ref[0, hd, :_NOPE, :] = (qt[hd * _NOPE:(hd + 1) * _NOPE] * qscale).astype(_BF16)
        r = qt[nh + hd * _ROPE:nh + (hd + 1) * _ROPE]
        rs = qt[nh + nr + hd * _ROPE:nh + nr + (hd + 1) * _ROPE]
        qt_ref[0, hd, _NOPE:, :] = ((r * ct + rs * st) * qscale).astype(_BF16)
        k_ref[0, hd, :, :_NOPE] = kn[:, hd * _NOPE:(hd + 1) * _NOPE].astype(_BF16)
        k_ref[0, hd, :, _NOPE:_QK] = k_rope
        k_ref[0, hd, :, _QK:] = k_pad
        for c in range(vt_ref.shape[2]):
            vt_ref[0, hd, c] = vt[hd * _VDIM:(hd + 1) * _VDIM, c * _KSUB:(c + 1) * _KSUB].astype(_BF16)


def _mla_proj(x2d, posc, posr, win, gq, gkv, wqt, wk, wvt, invf, invb, batch, seq, tm):
    t, d = x2d.shape
    spb = seq // tm
    qscale = (_QK ** -0.5) * math.log2(math.e)
    consts = [win, gq, gkv, wqt, wk, wvt, invf, invb]
    return pl.pallas_call(
        functools.partial(_mla_proj_kernel, qscale=qscale),
        grid=(t // tm,),
        in_specs=[
            pl.BlockSpec((tm, d), lambda i: (i, 0)),
            pl.BlockSpec((tm, 1), lambda i: (i, 0)),
            pl.BlockSpec((1, tm), lambda i: (0, i)),
        ] + [_const_spec(c.shape) for c in consts],
        out_specs=[
            pl.BlockSpec((1, _HEADS, _QK, tm), lambda i: (i // spb, 0, 0, i % spb)),
            pl.BlockSpec((1, _HEADS, tm, _QK + _KPAD), lambda i: (i // spb, 0, i % spb, 0)),
            pl.BlockSpec((1, _HEADS, tm // _KSUB, _VDIM, _KSUB), lambda i: (i // spb, 0, i % spb, 0, 0)),
        ],
        out_shape=[
            jax.ShapeDtypeStruct((batch, _HEADS, _QK, seq), _BF16),
            jax.ShapeDtypeStruct((batch, _HEADS, seq, _QK + _KPAD), _BF16),
            jax.ShapeDtypeStruct((batch, _HEADS, seq // _KSUB, _VDIM, _KSUB), _BF16),
        ],
        compiler_params=_params("parallel"),
        name="mla_proj",
    )(x2d, posc, posr, *consts)


_ONES_ROWS = _KPAD
_KW = _QK + _KPAD
_TAU = 20.0


def _attn_kernel(qt_ref, k_ref, vt_ref, o_ref, qx_ref, r_ref, p_ref, acc_ref, *, sub, nsub, group):
    tq = qt_ref.shape[-1]
    ones = (lax.broadcasted_iota(jnp.int32, (_ONES_ROWS, sub), 0) == 0).astype(_BF16)
    row = lax.broadcasted_iota(jnp.int32, (_ONES_ROWS, tq), 0)

    def set_reference(r):
        r_ref[...] = r
        qx_ref[_QK:, :] = jnp.where(row == 0, -r, 0.0).astype(_BF16)

    def shifted_scores(j):
        kj = k_ref[0, 0, pl.ds(pl.multiple_of(j * sub, sub), sub), :]
        return _dot(kj, qx_ref[...])

    def probs(j, slot):
        st = shifted_scores(j)
        p_ref[slot] = jnp.exp2(st).astype(_BF16)
        return jnp.max(st, axis=0, keepdims=True)

    def accumulate(j, slot):
        vte = jnp.concatenate([vt_ref[0, 0, j], ones], axis=0)
        acc_ref[...] += _dot(vte, p_ref[slot])

    qx_ref[:_QK, :] = qt_ref[0, 0]
    set_reference(jnp.zeros((1, tq), _F32))
    acc_ref[...] = jnp.zeros(acc_ref.shape, _F32)
    first = jnp.max(shifted_scores(0), axis=0, keepdims=True)
    set_reference(first.astype(_BF16).astype(_F32))

    def body(t, carry):
        base = t * group
        cm = probs(base, 0)
        for g in range(1, group):
            cm = jnp.maximum(cm, probs(base + g, g))

        @pl.when(jnp.max(cm) > _TAU)
        def _():
            r_old = r_ref[...]
            r_new = (r_old + jnp.maximum(cm, 0.0)).astype(_BF16).astype(_F32)
            acc_ref[...] = acc_ref[...] * jnp.exp2(r_old - r_new)
            set_reference(r_new)
            for g in range(group):
                probs(base + g, g)

        for g in range(group):
            accumulate(base + g, g)
        return carry

    lax.fori_loop(0, nsub // group, body, 0)
    acc = acc_ref[...]
    o = acc[:_VDIM] / acc[_VDIM:_VDIM + 1]
    o_ref[0] = o.T.astype(o_ref.dtype)


def _attention(qt, k, vt5, tq, group):
    b, h, s, kw = k.shape
    nsub, sub = vt5.shape[2], vt5.shape[4]
    assert nsub % group == 0
    return pl.pallas_call(
        functools.partial(_attn_kernel, sub=sub, nsub=nsub, group=group),
        grid=(b, h, s // tq),
        in_specs=[
            pl.BlockSpec((1, 1, _QK, tq), lambda bi, hi, qi: (bi, hi, 0, qi)),
            pl.BlockSpec((1, 1, s, kw), lambda bi, hi, qi: (bi, hi, 0, 0)),
            pl.BlockSpec((1, 1, nsub, _VDIM, sub), lambda bi, hi, qi: (bi, hi, 0, 0, 0)),
        ],
        out_specs=pl.BlockSpec((1, tq, _VDIM), lambda bi, hi, qi: (bi, qi, hi)),
        out_shape=jax.ShapeDtypeStruct((b, s, h * _VDIM), _BF16),
        scratch_shapes=[
            pltpu.VMEM((kw, tq), _BF16),
            pltpu.VMEM((1, tq), _F32),
            pltpu.VMEM((group, sub, tq), _BF16),
            pltpu.VMEM((_VDIM + _ONES_ROWS, tq), _F32),
        ],
        compiler_params=_params("parallel", "parallel", "parallel"),
        name="mla_attention",
    )(qt, k, vt5)


def _proj_ln_kernel(a_ref, x_ref, w_ref, g_ref, b_ref, o_ref):
    m = _dot(a_ref[...].astype(_BF16), w_ref[...])
    o_ref[...] = _layer_norm(_ALPHA * x_ref[...] + m, g_ref[...], b_ref[...])


def _proj_ln(a2d, x2d, w, g, b, tm):
    t, d = x2d.shape
    ka = a2d.shape[1]
    return pl.pallas_call(
        _proj_ln_kernel,
        grid=(t // tm,),
        in_specs=[
            pl.BlockSpec((tm, ka), lambda i: (i, 0)),
            pl.BlockSpec((tm, d), lambda i: (i, 0)),
            _const_spec(w.shape), _const_spec(g.shape), _const_spec(b.shape),
        ],
        out_specs=pl.BlockSpec((tm, d), lambda i: (i, 0)),
        out_shape=jax.ShapeDtypeStruct((t, d), _F32),
        compiler_params=_params("parallel"),
        name="proj_ln",
    )(a2d, x2d, w, g, b)


def _halo_specs(tm, d, nrows):
    r = tm // _SUBLANES
    last = nrows // _SUBLANES - 1
    return [
        pl.BlockSpec((_SUBLANES, d), lambda i: (jnp.maximum(i * r - 1, 0), 0)),
        pl.BlockSpec((tm, d), lambda i: (i, 0)),
        pl.BlockSpec((_SUBLANES, d), lambda i: (jnp.minimum((i + 1) * r, last), 0)),
    ]


def _assemble_halo(xp_ref, x_ref, xn_ref, xcat_ref, tm, tiles_per_seq):
    i = pl.program_id(0)
    first = (i % tiles_per_seq) == 0
    last = (i % tiles_per_seq) == tiles_per_seq - 1
    xcat_ref[0:_SUBLANES, :] = jnp.where(first, 0.0, xp_ref[...])
    xcat_ref[_SUBLANES:_SUBLANES + tm, :] = x_ref[...]
    xcat_ref[_SUBLANES + tm:, :] = jnp.where(last, 0.0, xn_ref[...])


def _dwconv3(h_ref, wc_ref, tm):
    lo = h_ref[pl.ds(_SUBLANES - 1, tm), :]
    mid = h_ref[pl.ds(_SUBLANES, tm), :]
    hi = h_ref[pl.ds(_SUBLANES + 1, tm), :]
    return lo * wc_ref[0:1, :] + mid * wc_ref[1:2, :] + hi * wc_ref[2:3, :]


def _conv_ffn_kernel(xp_ref, x_ref, xn_ref, wup_ref, wc_ref, wdn_ref, g_ref, b_ref, o_ref,
                     xcat_ref, h_ref, *, tm, tiles_per_seq, dff):
    _assemble_halo(xp_ref, x_ref, xn_ref, xcat_ref, tm, tiles_per_seq)
    h_ref[...] = _dot(xcat_ref[...].astype(_BF16), wup_ref[...])
    hc = _dwconv3(h_ref, wc_ref, tm)
    a = hc[:, :dff]
    gt = hc[:, dff:]
    act = (gt * jax.nn.sigmoid(gt) * a).astype(_BF16)
    f = _dot(act, wdn_ref[...])
    o_ref[...] = _layer_norm(_ALPHA * x_ref[...] + f, g_ref[...], b_ref[...])


def _conv_ffn(x2d, wup, wc, wdn, g, b, seq, tm):
    t, d = x2d.shape
    dff = wdn.shape[0]
    return pl.pallas_call(
        functools.partial(_conv_ffn_kernel, tm=tm, tiles_per_seq=seq // tm, dff=dff),
        grid=(t // tm,),
        in_specs=_halo_specs(tm, d, t) + [
            _const_spec(wup.shape), _const_spec(wc.shape), _const_spec(wdn.shape),
            _const_spec(g.shape), _const_spec(b.shape),
        ],
        out_specs=pl.BlockSpec((tm, d), lambda i: (i, 0)),
        out_shape=jax.ShapeDtypeStruct((t, d), _F32),
        scratch_shapes=[
            pltpu.VMEM((tm + 2 * _SUBLANES, d), _F32),
            pltpu.VMEM((tm + 2 * _SUBLANES, 2 * dff), _F32),
        ],
        compiler_params=_params("parallel"),
        name="conv_ffn",
    )(x2d, x2d, x2d, wup, wc, wdn, g, b)


def _hyena_in_kernel(xp_ref, x_ref, xn_ref, win_ref, wc_ref, v_ref, g1_ref, g2_ref,
                     xcat_ref, h_ref, *, tm, tiles_per_seq, d):
    _assemble_halo(xp_ref, x_ref, xn_ref, xcat_ref, tm, tiles_per_seq)
    h_ref[...] = _dot(xcat_ref[...].astype(_BF16), win_ref[...])
    u = _dwconv3(h_ref, wc_ref, tm)
    for a in range(tm // _N2):
        rows = slice(a * _N2, (a + 1) * _N2)
        for o_ref, c in ((v_ref, 0), (g1_ref, 1), (g2_ref, 2)):
            o_ref[:, 0, a] = u[rows, c * d:(c + 1) * d].reshape(_N2 // _SUBLANES, _SUBLANES, d)


def _hyena_in(x2d, win, wc, seq, tm):
    t, d = x2d.shape
    spb = seq // tm
    ng = _N2 // _SUBLANES
    out = jax.ShapeDtypeStruct((ng, t // seq, seq // _N2, _SUBLANES, d), _F32)
    ospec = pl.BlockSpec((ng, 1, tm // _N2, _SUBLANES, d), lambda i: (0, i // spb, i % spb, 0, 0))
    return pl.pallas_call(
        functools.partial(_hyena_in_kernel, tm=tm, tiles_per_seq=seq // tm, d=d),
        grid=(t // tm,),
        in_specs=_halo_specs(tm, d, t) + [_const_spec(win.shape), _const_spec(wc.shape)],
        out_specs=[ospec, ospec, ospec],
        out_shape=[out, out, out],
        scratch_shapes=[
            pltpu.VMEM((tm + 2 * _SUBLANES, d), _F32),
            pltpu.VMEM((tm + 2 * _SUBLANES, 3 * d), _F32),
        ],
        compiler_params=_params("parallel"),
        name="hyena_in",
    )(x2d, x2d, x2d, win, wc)


def _dft_consts(n1):
    n = n1 * _N2
    k = np.arange(n1)
    ang1 = -2.0 * np.pi * ((k[:, None] * k[None, :]) % n1) / n1
    f1r, f1i = np.cos(ang1), np.sin(ang1)
    half = n1 // 2
    fwd_c = np.block([[f1r[:, :half], -f1i[:, :half]], [f1i[:, :half], f1r[:, :half]]])
    fwd_r = np.concatenate([f1r, f1i], axis=0)
    inv_c = np.block([[f1r[:half], f1i[:half]], [-f1i[:half], f1r[:half]]]) / n
    k2 = np.arange(_N2)
    ang2 = -2.0 * np.pi * ((k2[:, None] * k2[None, :]) % _N2) / _N2
    angt = -2.0 * np.pi * ((k[:, None] * k2[None, :]) % n) / n
    f32 = lambda a: jnp.asarray(a.astype(np.float32))
    return dict(fwd_c=f32(fwd_c), fwd_r=f32(fwd_r), inv_c=f32(inv_c),
                f2r=f32(np.cos(ang2)), f2i=f32(np.sin(ang2)),
                twr=f32(np.cos(angt)), twi=f32(np.sin(angt)))


def _tile_row(ref, r):
    flat = ref.reshape(math.prod(ref.shape[:-1]), ref.shape[-1])
    return flat[pl.ds(r, flat.shape[0] // _SUBLANES, stride=_SUBLANES), :]


def _tiles(x):
    return x.reshape(x.shape[0] // _SUBLANES, _SUBLANES, x.shape[1])


def _filter_mlp_kernel(fl_ref, w1_ref, b1_ref, w2_ref, b2_ref, w3_ref, b3_ref, fr_ref, h_ref, *, rows, length):
    hn2 = _N2 // 2
    q = lax.broadcasted_iota(jnp.int32, (rows, 1), 0)
    r_lo = _N2 * (pl.program_id(0) * (rows // hn2) + q // hn2) + q % hn2
    lane = lax.broadcasted_iota(jnp.int32, (rows, _LANES), 1)
    phase = jnp.where(lane <= _HY_BANDS, 0.5 * math.pi, math.pi)

    def features(r):
        lag = jnp.where(r < length, r, 2 * length - r).astype(_F32)
        a = (lag * (2.0 * math.pi / length)) * fl_ref[...]
        z = jnp.where(lane == 0, lag / (length - 1.0), jnp.where(lane <= 2 * _HY_BANDS, jnp.sin(a + phase), 0.0))
        return z.astype(_BF16)

    fr = fr_ref[...]
    h = jnp.sin(fr * (_dot(features(r_lo), w1_ref[0]) + _dot(features(r_lo + hn2), w1_ref[1]) + b1_ref[...]))
    h = jnp.sin(fr * (_dot(h.astype(_BF16), w2_ref[...]) + b2_ref[...]))
    h = jnp.sin(fr * (_dot(h.astype(_BF16), w3_ref[...]) + b3_ref[...]))
    for a in range(rows // hn2):
        h_ref[:, a] = h[a * hn2:(a + 1) * hn2].reshape(hn2 // _SUBLANES, _SUBLANES, _LANES)


def _filter_mlp(fl, w1, b1, w2, b2, w3, b3, fr, length, rows):
    consts = [fl, w1, b1, w2, b2, w3, b3, fr]
    hn2 = _N2 // 2
    n1 = 2 * length // _N2
    return pl.pallas_call(
        functools.partial(_filter_mlp_kernel, rows=rows, length=length),
        grid=(n1 * hn2 // rows,),
        in_specs=[_const_spec(c.shape) for c in consts],
        out_specs=pl.BlockSpec((hn2 // _SUBLANES, rows // hn2, _SUBLANES, _LANES), lambda i: (0, i, 0, 0)),
        out_shape=jax.ShapeDtypeStruct((hn2 // _SUBLANES, n1, _SUBLANES, _LANES), _F32),
        compiler_params=_params("parallel"),
        name="hyena_filter_mlp",
    )(*consts)


def _filter_fft_kernel(h_ref, wo_ref, dl_ref, mf_ref, ak_ref, *, jb, length):
    n1 = h_ref.shape[1]
    half = n1 // 2
    j0 = pl.program_id(0) * jb
    row = lax.broadcasted_iota(jnp.int32, (n1, 1), 0)
    dc = dl_ref.shape[1]

    def filter_rows(jj, o):
        r = _N2 * row + (j0 + jj)
        lag = jnp.where(r < length, r, 2 * length - r).astype(_F32)
        decay = jnp.exp(-(lag / (length - 1.0)) * dl_ref[...])
        hb = _tile_row(h_ref.at[jj // _SUBLANES], jj % _SUBLANES).astype(_BF16)
        out = jnp.concatenate([_dot(hb[:half], wo_ref[0, o]), _dot(hb[half:], wo_ref[1, o])], axis=0)
        lag0 = _dot(hb[0:_SUBLANES], wo_ref[1, o])[0:1]
        ko = jnp.where(r != length, out * decay, 0.0) + jnp.where(r == 0, lag0 * decay[0:1], 0.0)
        return ko.astype(_BF16)

    for jj in range(0, jb, 2):
        for o in range(2):
            res = _dot(mf_ref[...], jnp.concatenate([filter_rows(jj, o), filter_rows(jj + 1, o)], axis=1))
            ak_ref[o, :, jj] = _tiles(res[:, :dc])
            ak_ref[o, :, jj + 1] = _tiles(res[:, dc:])


def _filter_fft(h4, wo4, dl, mf, length, jb, dc):
    ng, n1, _, hf = h4.shape
    n2 = 2 * ng * _SUBLANES
    nb = n2 // 2 // jb
    d = dl.shape[1]
    rows = mf.shape[0]
    return pl.pallas_call(
        functools.partial(_filter_fft_kernel, jb=jb, length=length),
        grid=(n2 // jb, d // dc),
        in_specs=[pl.BlockSpec((jb // _SUBLANES, n1, _SUBLANES, hf), lambda j, c: (j % nb, 0, 0, 0)),
                  pl.BlockSpec((None, 2, 2, hf, dc), lambda j, c: (j // nb, 0, 0, 0, c)),
                  pl.BlockSpec((1, dc), lambda j, c: (0, c)),
                  _const_spec(mf.shape)],
        out_specs=pl.BlockSpec((2, rows // _SUBLANES, jb, _SUBLANES, dc), lambda j, c: (0, 0, j, 0, c)),
        out_shape=jax.ShapeDtypeStruct((2, rows // _SUBLANES, n2, _SUBLANES, d), _F32),
        compiler_params=_params("parallel", "parallel"),
        name="hyena_filter_fft",
    )(h4, wo4, dl, mf)


def _outer_fwd_kernel(m_ref, x_ref, o_ref, *, jb):
    dc = x_ref.shape[-1]
    pick = lambda jj: _tile_row(x_ref.at[jj // _SUBLANES], jj % _SUBLANES)
    for jj in range(0, jb, 2):
        res = _dot(m_ref[...], jnp.concatenate([pick(jj), pick(jj + 1)], axis=1).astype(_BF16))
        o_ref[:, jj] = _tiles(res[:, :dc])
        o_ref[:, jj + 1] = _tiles(res[:, dc:])


def _outer_fwd(m, x5, jb, dc):
    ng, b, half, _, d = x5.shape
    n2 = ng * _SUBLANES
    rows = m.shape[0]
    return pl.pallas_call(
        functools.partial(_outer_fwd_kernel, jb=jb),
        grid=(n2 // jb, d // dc),
        in_specs=[_const_spec(m.shape),
                  pl.BlockSpec((jb // _SUBLANES, b, half, _SUBLANES, dc), lambda j, c: (j, 0, 0, 0, c))],
        out_specs=pl.BlockSpec((rows // _SUBLANES, jb, _SUBLANES, dc), lambda j, c: (0, j, 0, c)),
        out_shape=jax.ShapeDtypeStruct((rows // _SUBLANES, n2, _SUBLANES, d), _F32),
        compiler_params=_params("parallel", "parallel"),
        name="fft_outer_fwd",
    )(m, x5)


def _mid_kernel(*refs, kb, nl):
    ar, ai, kr, ki = (refs[t * nl:(t + 1) * nl] for t in range(4))
    f2r_ref, f2i_ref, twr_ref, twi_ref, b_ref, m2_ref, m2t_ref = refs[4 * nl:]
    i = pl.program_id(0)

    @pl.when(pl.program_id(1) == 0)
    def _():
        f2r = f2r_ref[...]
        f2i = f2i_ref[...]
        for j in range(kb):
            k1 = i * kb + j
            tr = twr_ref[pl.ds(k1, 1), :]
            ti = twi_ref[pl.ds(k1, 1), :]
            cr = f2r * tr - f2i * ti
            ci = f2r * ti + f2i * tr
            m2 = jnp.concatenate([jnp.concatenate([cr, -ci], axis=1),
                                  jnp.concatenate([ci, cr], axis=1)], axis=0)
            m2_ref[j] = m2.astype(_BF16)
            m2t_ref[j] = m2.T.astype(_BF16)

    for j in range(kb):
        pick = lambda rs: jnp.concatenate([_tile_row(r.at[j // _SUBLANES], j % _SUBLANES) for r in rs], axis=1)
        a = jnp.concatenate([pick(ar), pick(ai)], axis=0).astype(_BF16)
        ak = jnp.concatenate([pick(kr), pick(ki)], axis=0).astype(_BF16)
        x = _dot(m2_ref[j], a)
        kf = _dot(m2_ref[j], ak)
        xr, xi = x[:_N2], x[_N2:]
        kfr, kfi = kf[:_N2], kf[_N2:]
        y = jnp.concatenate([xr * kfr - xi * kfi, xr * kfi + xi * kfr], axis=0).astype(_BF16)
        bt = _dot(m2t_ref[j], y)
        b_ref[:, 0, j] = _tiles(bt[:_N2])
        b_ref[:, 1, j] = _tiles(bt[_N2:])


def _mid(a4, ak5, order, c, kb, nl):
    groups, n2, _, d = a4.shape
    n1 = groups * _SUBLANES // 2
    nk = n1 // kb
    kg = kb // _SUBLANES

    def chunks(block, index):
        return [pl.BlockSpec(block + (_LANES,), functools.partial(index, l)) for l in range(nl)]

    sig = (kg, n2, _SUBLANES)
    in_specs = (chunks(sig, lambda l, i, j: (i, 0, 0, nl * j + l))
                + chunks(sig, lambda l, i, j: (nk + i, 0, 0, nl * j + l))
                + chunks((None,) + sig, lambda l, i, j: (order, i, 0, 0, nl * j + l))
                + chunks((None,) + sig, lambda l, i, j: (order, nk + i, 0, 0, nl * j + l))
                + [_const_spec(c[name].shape) for name in ("f2r", "f2i", "twr", "twi")])
    return pl.pallas_call(
        functools.partial(_mid_kernel, kb=kb, nl=nl),
        grid=(nk, d // (nl * _LANES)),
        in_specs=in_specs,
        out_specs=pl.BlockSpec((n2 // _SUBLANES, 2, kb, _SUBLANES, nl * _LANES), lambda i, j: (0, 0, i, 0, j)),
        out_shape=jax.ShapeDtypeStruct((n2 // _SUBLANES, 2, n1, _SUBLANES, d), _F32),
        scratch_shapes=[pltpu.VMEM((kb, 2 * n2, 2 * n2), _BF16), pltpu.VMEM((kb, 2 * n2, 2 * n2), _BF16)],
        compiler_params=_params("parallel", "arbitrary"),
        name="fft_mid",
    )(*([a4] * (2 * nl) + [ak5] * (2 * nl)), c["f2r"], c["f2i"], c["twr"], c["twi"])


def _outer_inv_kernel(minv_ref, mfwd_ref, b_ref, z_ref, gate_ref, bias_ref, zo_ref, ao_ref, *, jb, z_slabs):
    dc = b_ref.shape[-1]
    time_order = ao_ref is None
    cols = []
    pair = lambda f, jj: jnp.concatenate([f(jj), f(jj + 1)], axis=1)
    row_of = lambda ref: (lambda q: _tile_row(ref.at[q // _SUBLANES], q % _SUBLANES))
    bias = jnp.concatenate([bias_ref[...], bias_ref[...]], axis=1)
    for jj in range(0, jb, 2):
        bj = pair(row_of(b_ref), jj).astype(_BF16)
        y = _dot(minv_ref[...], bj)
        zj = pair((lambda q: z_ref[q]) if z_slabs else row_of(z_ref), jj)
        zn = pair(row_of(gate_ref), jj) * (y + bias * zj)
        if time_order:
            cols += [zn[:, :dc], zn[:, dc:]]
        else:
            zo_ref[jj] = zn[:, :dc]
            zo_ref[jj + 1] = zn[:, dc:]
        if ao_ref is not None:
            res = _dot(mfwd_ref[...], zn.astype(_BF16))
            ao_ref[:, jj] = _tiles(res[:, :dc])
            ao_ref[:, jj + 1] = _tiles(res[:, dc:])
    if time_order:
        zo_ref[...] = jnp.stack(cols, axis=1).reshape(zo_ref.shape)


def _outer_inv_last_kernel(minv_ref, b_ref, z_ref, gate_ref, bias_ref, zo_ref, *, jb, z_slabs):
    _outer_inv_kernel(minv_ref, None, b_ref, z_ref, gate_ref, bias_ref, zo_ref, None, jb=jb, z_slabs=z_slabs)


def _outer_inv(minv, mfwd, b5, z, gate5, bias, jb, dc):
    ng, b, half, _, d = gate5.shape
    n2 = ng * _SUBLANES
    jg = jb // _SUBLANES
    z_slabs = z.ndim == 3
    gspec = pl.BlockSpec((jg, b, half, _SUBLANES, dc), lambda j, c: (j, 0, 0, 0, c))
    zslab = pl.BlockSpec((jb, b * half, dc), lambda j, c: (j, 0, c))
    common = [pl.BlockSpec((jg,) + b5.shape[1:4] + (dc,), lambda j, c: (j, 0, 0, 0, c)),
              zslab if z_slabs else gspec, gspec, pl.BlockSpec((1, dc), lambda j, c: (0, c))]
    zshape = jax.ShapeDtypeStruct((n2, b * half, d), _F32)
    if mfwd is None:
        return pl.pallas_call(
            functools.partial(_outer_inv_last_kernel, jb=jb, z_slabs=z_slabs),
            grid=(n2 // jb, d // dc),
            in_specs=[_const_spec(minv.shape)] + common,
            out_specs=pl.BlockSpec((b, half, jb, dc), lambda j, c: (0, 0, j, c)),
            out_shape=jax.ShapeDtypeStruct((b, half, n2, d), _F32),
            compiler_params=_params("parallel", "parallel"),
            name="fft_outer_inv_last",
        )(minv, b5, z, gate5, bias)
    rows = mfwd.shape[0]
    return pl.pallas_call(
        functools.partial(_outer_inv_kernel, jb=jb, z_slabs=z_slabs),
        grid=(n2 // jb, d // dc),
        in_specs=[_const_spec(minv.shape), _const_spec(mfwd.shape)] + common,
        out_specs=[zslab, pl.BlockSpec((rows // _SUBLANES, jb, _SUBLANES, dc), lambda j, c: (0, j, 0, c))],
        out_shape=[zshape, jax.ShapeDtypeStruct((rows // _SUBLANES, n2, _SUBLANES, d), _F32)],
        compiler_params=_params("parallel", "parallel"),
        name="fft_outer_inv_fwd",
    )(minv, mfwd, b5, z, gate5, bias)


def _pick(n, target):
    t = min(n, target)
    while n % t:
        t //= 2
    return t


def _hyena_mixer(x2d, batch, seq, w_in, w_short, fw1, fb1, fw2, fb2, fw3, fb3, freq, fw_out, d_bias):
    assert batch == 2, "batch 0 / batch 1 are packed as real / imaginary parts of one FFT"
    t, d = x2d.shape
    n1 = 2 * seq // _N2
    half = n1 // 2
    hf = fw1.shape[1]
    tm = _pick(seq, 512)
    v, g1, g2 = _hyena_in(x2d, w_in.astype(_BF16), w_short, seq, tm)

    c = _dft_consts(n1)
    fwd_c = c["fwd_c"].astype(_BF16)
    fwd_r = c["fwd_r"].astype(_BF16)
    inv_c = c["inv_c"].astype(_BF16)
    jb = 2 * _SUBLANES
    dc = _LANES

    fl = jnp.concatenate([jnp.zeros((1,), _F32),
                          jnp.linspace(1e-4, _HY_BANDS - 1, _HY_BANDS, dtype=_F32),
                          jnp.linspace(1e-4, _HY_BANDS - 1, _HY_BANDS, dtype=_F32),
                          jnp.zeros((_LANES - _HY_EMB,), _F32)])[None, :]
    assert 2 * hf == _LANES
    w1z = jnp.zeros((_LANES, _LANES), _F32)
    w1p = jnp.stack([w1z.at[:_HY_EMB, :hf].set(fw1), w1z.at[:_HY_EMB, hf:].set(fw1)]).astype(_BF16)
    twice = lambda a: jnp.tile(a[None, :], (1, 2))
    blockdiag = lambda w: jnp.zeros((_LANES, _LANES), _F32).at[:hf, :hf].set(w).at[hf:, hf:].set(w).astype(_BF16)
    deltas = jnp.abs(jnp.linspace(math.log(_HY_FAST) / _HY_TARGET, math.log(_HY_SLOW) / _HY_TARGET,
                                  d, dtype=_F32))[None, :]
    h3 = _filter_mlp(fl, w1p, twice(fb1), blockdiag(fw2), twice(fb2), blockdiag(fw3), twice(fb3), twice(freq),
                     seq, _pick(seq, 1024))
    wo = jnp.transpose(fw_out.reshape(hf, 2, 2, d), (2, 1, 0, 3))
    woz = jnp.zeros_like(wo)
    wo5 = jnp.stack([jnp.concatenate([wo, woz], axis=2), jnp.concatenate([woz, wo], axis=2)]).astype(_BF16)
    ak = _filter_fft(h3, wo5, deltas, fwd_r, seq, jb, dc)

    kb = _SUBLANES
    nl = _pick(d // _LANES, 4)
    z = v
    gates = (g1, g2)
    a = _outer_fwd(fwd_c, z, jb, dc)
    for o in range(2):
        bsp = _mid(a, ak, o, c, kb, nl)
        bias = d_bias[o][None, :]
        if o == 0:
            z, a = _outer_inv(inv_c, fwd_c, bsp, z, gates[o], bias, jb, dc)
        else:
            z = _outer_inv(inv_c, None, bsp, z, gates[o], bias, jb, dc)
    return z.reshape(t, d)


def _mla_weights(w_in, w_uq, w_ukv):
    d = w_in.shape[0]
    r0 = _Q_LORA + _KV_LORA
    hr = _ROPE // 2
    zpad = jnp.zeros((d, 64), w_in.dtype)
    win = jnp.concatenate([w_in, zpad, w_in[:, r0 + hr:r0 + _ROPE], w_in[:, r0:r0 + hr], zpad], axis=1)
    wq3 = w_uq.reshape(_Q_LORA, _HEADS, _QK)
    wq_n = wq3[:, :, :_NOPE].reshape(_Q_LORA, _HEADS * _NOPE)
    wq_r = wq3[:, :, _NOPE:].reshape(_Q_LORA, _HEADS * _ROPE)
    wq_s = jnp.concatenate([wq3[:, :, _NOPE + hr:], wq3[:, :, _NOPE:_NOPE + hr]], axis=2)
    wqt = jnp.concatenate([wq_n, wq_r, wq_s.reshape(_Q_LORA, _HEADS * _ROPE)], axis=1).T
    wkv3 = w_ukv.reshape(_KV_LORA, _HEADS, _NOPE + _VDIM)
    wk = wkv3[:, :, :_NOPE].reshape(_KV_LORA, _HEADS * _NOPE)
    wvt = wkv3[:, :, _NOPE:].reshape(_KV_LORA, _HEADS * _VDIM).T
    return win.astype(_BF16), wqt.astype(_BF16), wk.astype(_BF16), wvt.astype(_BF16)


def _mla_mixer(x2d, positions, batch, seq, w_in, g_q, w_uq, g_kv, w_ukv):
    win, wqt, wk, wvt = _mla_weights(w_in, w_uq, w_ukv)
    inv = 1.0 / (_ROPE_THETA ** (jnp.arange(0, _ROPE, 2, dtype=_F32) / _ROPE))
    tm = _pick(seq, 512)
    invf = jnp.tile(inv, 2 * _LANES // _ROPE)[None, :]
    invb = jnp.broadcast_to(jnp.tile(inv, 2)[:, None], (_ROPE, tm))
    qt, k, vt = _mla_proj(x2d, positions.reshape(-1, 1), positions.reshape(1, -1), win,
                          g_q[None, :], g_kv[None, :], wqt, wk, wvt, invf, invb, batch, seq, tm)
    o = _attention(qt, k, vt, _pick(seq, 1024), _pick(seq // _KSUB, 32))
    return o.reshape(batch * seq, _HEADS * _VDIM)


def kernel(x, positions, mla_w_in, mla_g_q, mla_w_uq, mla_g_kv, mla_w_ukv, mla_w_o, hy_w_in, hy_w_short, hy_fw1, hy_fb1, hy_fw2, hy_fb2, hy_fw3, hy_fb3, hy_freq, hy_fw_out, hy_d_bias, hy_w_o, ffn_w_up, ffn_w_conv, ffn_w_down, ln1_g, ln1_b, ln2_g, ln2_b):
    batch, seq, d = x.shape
    x2d = x.reshape(batch * seq, d)
    tm = _pick(seq, 512)
    for i in range(_DEPTH):
        j = i // 2
        if i % 2 == 0:
            a = _mla_mixer(x2d, positions, batch, seq, mla_w_in[j], mla_g_q[j], mla_w_uq[j],
                           mla_g_kv[j], mla_w_ukv[j])
            w_o = mla_w_o[j]
        else:
            a = _hyena_mixer(x2d, batch, seq, hy_w_in[j], hy_w_short[j], hy_fw1[j], hy_fb1[j], hy_fw2[j],
                             hy_fb2[j], hy_fw3[j], hy_fb3[j], hy_freq[j], hy_fw_out[j], hy_d_bias[j])
            w_o = hy_w_o[j]
        x2d = _proj_ln(a, x2d, w_o.astype(_BF16), ln1_g[i][None, :], ln1_b[i][None, :], tm)
        x2d = _conv_ffn(x2d, ffn_w_up[i].astype(_BF16), ffn_w_conv[i], ffn_w_down[i].astype(_BF16),
                        ln2_g[i][None, :], ln2_b[i][None, :], seq, tm)
    return x2d.reshape(batch, seq, d)
```
